```python
import math
import jax
import jax.numpy as jnp
from jax import lax
import numpy as np

D_MODEL = 4096
BATCH = 4
SEQ = 2048
DEPTH = 4
DEC_BATCH = 8
DEC_SEQ = 1
PAST_LEN = 8192
PAGE_SIZE = 128

HEAD_DIM = 128
CONV_DIM = D_MODEL // 4
CONV_W = 3
N_HEADS = (D_MODEL - CONV_DIM) // HEAD_DIM
N_KV = 4
GQA = N_HEADS // N_KV
ATTN_DIM = N_HEADS * HEAD_DIM
MIX_DIM = CONV_DIM + ATTN_DIM
KV_DIM = N_KV * HEAD_DIM
N_IN = 3 * CONV_DIM + ATTN_DIM + 6 * KV_DIM + 3 * N_HEADS
ROT_DIM = HEAD_DIM // 4
ROPE_THETA = 500000.0
BLK_CMP = 32
STRIDE_CMP = 16
BLK_SEL = 64
N_SEL = 16
WINDOW = 512
D_FF = 4 * D_MODEL
ALPHA = (2.0 * DEPTH) ** 0.25
BETA = (8.0 * DEPTH) ** -0.25
LN_EPS = 1e-5
Q_BLOCK = 128
SEL_Q_CHUNK = 32
FORCE = 1e4

kernel_name = 'nsa_shortconv_hybrid_decoder'


def layer_norm(x, g, b):
    xf = x.astype(jnp.float32)
    mu = xf.mean(-1, keepdims=True)
    var = jnp.square(xf - mu).mean(-1, keepdims=True)
    return ((xf - mu) * lax.rsqrt(var + LN_EPS) * g + b).astype(x.dtype)


def partial_rope(x, pos):
    half = ROT_DIM // 2
    inv = ROPE_THETA ** (-jnp.arange(half, dtype=jnp.float32) / half)
    ang = pos.astype(jnp.float32)[:, None] * inv[None, :]
    cos = jnp.cos(ang)[:, None, :]
    sin = jnp.sin(ang)[:, None, :]
    x1 = x[..., :half].astype(jnp.float32)
    x2 = x[..., half:ROT_DIM].astype(jnp.float32)
    rot = jnp.concatenate([x1 * cos - x2 * sin, x2 * cos + x1 * sin], -1).astype(x.dtype)
    return jnp.concatenate([rot, x[..., ROT_DIM:]], -1)


def masked_softmax(s, mask):
    s = jnp.where(mask, s.astype(jnp.float32), -jnp.inf)
    m = jnp.max(s, -1, keepdims=True)
    m = jnp.where(jnp.isfinite(m), m, 0.0)
    e = jnp.exp(s - m)
    return e / jnp.maximum(e.sum(-1, keepdims=True), 1e-30)


def short_conv(u, buf, w):
    ext = jnp.concatenate([buf, u], 1)
    nq = u.shape[1]
    y = w[0] * ext[:, 0:nq]
    for k in range(1, CONV_W):
        y = y + w[k] * ext[:, k:k + nq]
    return y, ext[:, ext.shape[1] - (CONV_W - 1):]


def compressed_and_select(q, kc, vc, qpos, pe, w1, w2):
    t_len = kc.shape[1]
    n_cmp = (t_len - BLK_CMP) // STRIDE_CMP + 1
    starts = jnp.arange(n_cmp, dtype=jnp.int32) * STRIDE_CMP
    idx = starts[:, None] + jnp.arange(BLK_CMP, dtype=jnp.int32)[None, :]

    def phi(rows, j):
        blk = rows[:, idx] + pe[j][None, None, :, None, :]
        h = jax.nn.gelu(jnp.einsum('bnlkd,lde->bnke', blk, w1[j]))
        return jnp.einsum('bnke,ef->bnkf', h, w2[j])

    k_cmp = phi(kc, 0)
    v_cmp = phi(vc, 1)
    s = jnp.einsum('bqkgd,bnkd->bqkgn', q, k_cmp)
    valid = (starts + BLK_CMP - 1)[None, :] <= qpos[:, None]
    p_cmp = masked_softmax(s, valid[None, :, None, None, :])
    o_cmp = jnp.einsum('bqkgn,bnkd->bqkgd', p_cmp.astype(vc.dtype), v_cmp)

    n_sel = -(-t_len // BLK_SEL)
    sel_start = jnp.arange(n_sel, dtype=jnp.int32) * BLK_SEL
    overlap = ((starts[:, None] < sel_start[None, :] + BLK_SEL)
               & (starts[:, None] + BLK_CMP > sel_start[None, :])).astype(jnp.float32)
    imp = jnp.einsum('bqkgn,nj->bqkj', p_cmp, overlap)
    jq = qpos // BLK_SEL
    jj = jnp.arange(n_sel, dtype=jnp.int32)[None, :]
    sel_valid = jj * BLK_SEL <= qpos[:, None]
    forced = (jj == 0) | (jj == jq[:, None]) | (jj == jq[:, None] - 1)
    score = jnp.where(sel_valid[None, :, None, :],
                      jnp.where(forced[None, :, None, :], FORCE, imp), -FORCE)
    _, sel_idx = lax.top_k(score, min(N_SEL, n_sel))
    return o_cmp, sel_idx


def selected_branch(q, ks, vs, qpos, sel_idx):
    bsz, t_len = ks.shape[:2]
    nq = q.shape[1]
    n_sel = -(-t_len // BLK_SEL)
    pad = n_sel * BLK_SEL - t_len

    def to_blocks(r):
        r = jnp.pad(r, ((0, 0), (0, pad), (0, 0), (0, 0)))
        return r.reshape(bsz, n_sel, BLK_SEL, N_KV, HEAD_DIM).transpose(0, 3, 1, 2, 4)

    ks_b = to_blocks(ks)
    vs_b = to_blocks(vs)
    n_top = sel_idx.shape[-1]
    qc = math.gcd(nq, SEL_Q_CHUNK)
    nc = nq // qc
    bi = jnp.arange(bsz)[:, None, None, None]
    hi = jnp.arange(N_KV)[None, None, :, None]

    def chunks(a):
        return jnp.moveaxis(a.reshape(bsz, nc, qc, *a.shape[2:]), 1, 0)

    def one(args):
        q_c, idx_c, pos_c = args
        k_g = ks_b[bi, hi, idx_c]
        v_g = vs_b[bi, hi, idx_c]
        s = jnp.einsum('bqhgd,bqhnld->bqhgnl', q_c, k_g).reshape(bsz, qc, N_KV, GQA, n_top * BLK_SEL)
        kpos = idx_c[..., None] * BLK_SEL + jnp.arange(BLK_SEL, dtype=jnp.int32)
        mask = (kpos <= pos_c[None, :, None, None, None]).reshape(bsz, qc, N_KV, 1, n_top * BLK_SEL)
        p = masked_softmax(s, mask)
        return jnp.einsum('bqhgm,bqhmd->bqhgd', p.astype(vs.dtype),
                          v_g.reshape(bsz, qc, N_KV, n_top * BLK_SEL, HEAD_DIM))

    o = lax.map(one, (chunks(q), chunks(sel_idx), qpos.reshape(nc, qc)))
    return jnp.moveaxis(o, 0, 1).reshape(bsz, nq, N_KV, GQA, HEAD_DIM)


def window_branch(q, k_ext, v_ext, t0):
    bsz, nq = q.shape[:2]
    pad = WINDOW - (k_ext.shape[1] - nq)
    k_ext = jnp.pad(k_ext, ((0, 0), (pad, 0), (0, 0), (0, 0)))
    v_ext = jnp.pad(v_ext, ((0, 0), (pad, 0), (0, 0), (0, 0)))
    qb = math.gcd(nq, Q_BLOCK)
    nb = nq // qb

    def one(b):
        q_b = lax.dynamic_slice_in_dim(q, b * qb, qb, axis=1)
        k_b = lax.dynamic_slice_in_dim(k_ext, b * qb, qb + WINDOW, axis=1)
        v_b = lax.dynamic_slice_in_dim(v_ext, b * qb, qb + WINDOW, axis=1)
        qp = t0 + b * qb + jnp.arange(qb, dtype=jnp.int32)
        kp = t0 - WINDOW + b * qb + jnp.arange(qb + WINDOW, dtype=jnp.int32)
        d = qp[:, None] - kp[None, :]
        mask = (d >= 0) & (d < WINDOW) & (kp[None, :] >= 0)
        s = jnp.einsum('bqhgd,bmhd->bqhgm', q_b, k_b)
        p = masked_softmax(s, mask[None, :, None, None, :])
        return jnp.einsum('bqhgm,bmhd->bqhgd', p.astype(v_b.dtype), v_b)

    o = lax.map(one, jnp.arange(nb))
    return jnp.moveaxis(o, 0, 1).reshape(bsz, nq, N_KV, GQA, HEAD_DIM)


def trunk_layer(x, past_cmp, past_sel, win_buf, conv_buf, w_in, conv_w, cmp_pe, cmp_w1, cmp_w2,
                w_out, ln1_g, ln1_b, w_up, w_down, ln2_g, ln2_b):
    bsz, nq, _ = x.shape
    p_len = past_cmp.shape[1]
    qpos = p_len + jnp.arange(nq, dtype=jnp.int32)
    proj = x @ w_in
    cuts = np.cumsum([CONV_DIM] * 3 + [ATTN_DIM] + [KV_DIM] * 6).tolist()
    gb, gc, hv, q, kc, vc, ks, vs, kw, vw, gates = jnp.split(proj, cuts, axis=-1)

    y_conv, new_conv = short_conv(gc * hv, conv_buf, conv_w)
    y_conv = gb * y_conv

    def kv_heads(a):
        return a.reshape(bsz, nq, N_KV, HEAD_DIM)

    q = partial_rope(q.reshape(bsz, nq, N_HEADS, HEAD_DIM), qpos)
    q = q.reshape(bsz, nq, N_KV, GQA, HEAD_DIM) * (HEAD_DIM ** -0.5)
    new_cmp = jnp.stack([partial_rope(kv_heads(kc), qpos), kv_heads(vc)], axis=2)
    new_sel = jnp.stack([partial_rope(kv_heads(ks), qpos), kv_heads(vs)], axis=2)
    new_win_rows = jnp.stack([partial_rope(kv_heads(kw), qpos), kv_heads(vw)], axis=2)
    all_cmp = jnp.concatenate([past_cmp, new_cmp], 1)
    all_sel = jnp.concatenate([past_sel, new_sel], 1)
    win_ext = jnp.concatenate([win_buf, new_win_rows], 1)
    o_cmp, sel_idx = compressed_and_select(q, all_cmp[:, :, 0], all_cmp[:, :, 1], qpos,
                                           cmp_pe, cmp_w1, cmp_w2)
    o_sel = selected_branch(q, all_sel[:, :, 0], all_sel[:, :, 1], qpos, sel_idx)
    o_win = window_branch(q, win_ext[:, :, 0], win_ext[:, :, 1], p_len)
    g = jax.nn.sigmoid(gates).reshape(bsz, nq, N_KV, GQA, 3)
    o_attn = (g[..., 0:1] * o_cmp + g[..., 1:2] * o_sel + g[..., 2:3] * o_win).reshape(bsz, nq, ATTN_DIM)

    mix = jnp.concatenate([y_conv, o_attn], -1) @ w_out
    x = layer_norm(ALPHA * x + mix, ln1_g, ln1_b)
    h = jnp.square(jax.nn.relu(x @ w_up)) @ w_down
    x = layer_norm(ALPHA * x + h, ln2_g, ln2_b)
    n_keep = win_buf.shape[1] if win_buf.shape[1] > 0 else min(WINDOW, nq)
    new_win = win_ext[:, win_ext.shape[1] - n_keep:]
    return x, new_cmp, new_sel, new_win, new_conv


def setup_inputs(seed: int = 0) -> dict:
    key = jax.random.key(seed)
    ks = jax.random.split(key, 20)
    n_pages = PAST_LEN // PAGE_SIZE
    n_used = DEC_BATCH * n_pages
    n_pool = n_used + n_used // 4
    win_buf = min(WINDOW, PAST_LEN)
    nrm = jax.random.normal
    f32 = jnp.float32
    page_table = jax.random.permutation(ks[0], n_pool)[:n_used].reshape(DEC_BATCH, n_pages).astype(jnp.int32)
    return {
        'x_prompt': nrm(ks[1], (BATCH, SEQ, D_MODEL), f32),
        'x_sample': nrm(ks[2], (DEC_BATCH, DEC_SEQ, D_MODEL), f32),
        'cache_cmp': nrm(ks[3], (DEPTH, n_pool, PAGE_SIZE, 2, N_KV, HEAD_DIM), f32),
        'cache_sel': nrm(ks[4], (DEPTH, n_pool, PAGE_SIZE, 2, N_KV, HEAD_DIM), f32),
        'state_win': nrm(ks[5], (DEPTH, DEC_BATCH, win_buf, 2, N_KV, HEAD_DIM), f32),
        'state_conv': nrm(ks[6], (DEPTH, DEC_BATCH, CONV_W - 1, CONV_DIM), f32),
        'page_table': page_table,
        'w_in': nrm(ks[7], (DEPTH, D_MODEL, N_IN), f32) * D_MODEL ** -0.5,
        'conv_w': nrm(ks[8], (DEPTH, CONV_W, CONV_DIM), f32) * CONV_W ** -0.5,
        'cmp_pe': nrm(ks[9], (DEPTH, 2, BLK_CMP, HEAD_DIM), f32) * 0.1,
        'cmp_w1': nrm(ks[10], (DEPTH, 2, BLK_CMP, HEAD_DIM, HEAD_DIM), f32) * (BLK_CMP * HEAD_DIM) ** -0.5,
        'cmp_w2': nrm(ks[11], (DEPTH, 2, HEAD_DIM, HEAD_DIM), f32) * HEAD_DIM ** -0.5,
        'w_out': nrm(ks[12], (DEPTH, MIX_DIM, D_MODEL), f32) * (BETA * MIX_DIM ** -0.5),
        'ln1_g': 1.0 + 0.01 * nrm(ks[13], (DEPTH, D_MODEL), f32),
        'ln1_b': 0.01 * nrm(ks[14], (DEPTH, D_MODEL), f32),
        'w_up': nrm(ks[15], (DEPTH, D_MODEL, D_FF), f32) * D_MODEL ** -0.5,
        'w_down': nrm(ks[16], (DEPTH, D_FF, D_MODEL), f32) * (BETA * D_FF ** -0.5),
        'ln2_g': 1.0 + 0.01 * nrm(ks[17], (DEPTH, D_MODEL), f32),
        'ln2_b': 0.01 * nrm(ks[18], (DEPTH, D_MODEL), f32),
    }


def reference(x_prompt, x_sample, cache_cmp, cache_sel, state_win, state_conv, page_table,
              w_in, conv_w, cmp_pe, cmp_w1, cmp_w2, w_out, ln1_g, ln1_b, w_up, w_down, ln2_g, ln2_b):
    n_pages = page_table.shape[1]
    past_len = n_pages * cache_cmp.shape[2]
    dec_b = x_sample.shape[0]
    bsz = x_prompt.shape[0]
    empty_kv = jnp.zeros((bsz, 0, 2, N_KV, HEAD_DIM), x_prompt.dtype)
    zero_conv = jnp.zeros((bsz, CONV_W - 1, CONV_DIM), x_prompt.dtype)
    xp, xs = x_prompt, x_sample
    cmp_p, sel_p, win_p, conv_p = [], [], [], []
    cmp_s, sel_s, win_s, conv_s = [], [], [], []
    for l in range(DEPTH):
        params = (w_in[l], conv_w[l], cmp_pe[l], cmp_w1[l], cmp_w2[l], w_out[l],
                  ln1_g[l], ln1_b[l], w_up[l], w_down[l], ln2_g[l], ln2_b[l])
        xp, c, s, w, cv = trunk_layer(xp, empty_kv, empty_kv, empty_kv, zero_conv, *params)
        cmp_p.append(c); sel_p.append(s); win_p.append(w); conv_p.append(cv)
        past_cmp = cache_cmp[l][page_table].reshape(dec_b, past_len, 2, N_KV, HEAD_DIM)
        past_sel = cache_sel[l][page_table].reshape(dec_b, past_len, 2, N_KV, HEAD_DIM)
        xs, c, s, w, cv = trunk_layer(xs, past_cmp, past_sel, state_win[l], state_conv[l], *params)
        cmp_s.append(c); sel_s.append(s); win_s.append(w); conv_s.append(cv)
    return (xp, xs, jnp.stack(cmp_p), jnp.stack(sel_p), jnp.stack(win_p), jnp.stack(conv_p),
            jnp.stack(cmp_s), jnp.stack(sel_s), jnp.stack(win_s), jnp.stack(conv_s))
```

```python
import functools
import math

import jax
import jax.numpy as jnp
from jax import lax
from jax.experimental import pallas as pl
from jax.experimental.pallas import tpu as pltpu

F32 = jnp.float32
BF16 = jnp.bfloat16

HEAD_DIM = 128
N_KV = 4
KV_DIM = N_KV * HEAD_DIM
ROT_DIM = HEAD_DIM // 4
ROPE_THETA = 500000.0
BLK_CMP = 32
STRIDE_CMP = 16
BLK_SEL = 64
N_SEL = 16
WINDOW = 512
CONV_W = 3
LN_EPS = 1e-5
FORCE = 1e4
NEG = -1e30

VMEM_LIMIT_BYTES = 56 * 1024 * 1024


def _cparams(sem):
    return pltpu.CompilerParams(dimension_semantics=sem, vmem_limit_bytes=VMEM_LIMIT_BYTES)


def _pick(n, pref):
    if n <= pref:
        return n
    t = pref
    while n % t:
        t //= 2
    return t


def _mm_kernel(*refs, nk, act, alpha, has_res):
    x_ref, w_ref = refs[0], refs[1]
    res_ref = refs[2] if has_res else None
    o_ref = refs[2 + has_res]
    acc_ref = refs[3 + has_res] if nk > 1 else None
    part = jnp.dot(x_ref[...], w_ref[...], preferred_element_type=F32)

    def finish(acc):
        if act == "relu2":
            acc = jnp.square(jnp.maximum(acc, 0.0))
        if has_res:
            acc = alpha * res_ref[...] + acc
        o_ref[...] = acc.astype(o_ref.dtype)

    if nk == 1:
        finish(part)
    else:
        k = pl.program_id(2)

        @pl.when(k == 0)
        def _():
            acc_ref[...] = part

        @pl.when(k > 0)
        def _():
            acc_ref[...] += part

        @pl.when(k == nk - 1)
        def _():
            finish(acc_ref[...])


def _matmul(x, w, *, out_dtype, act=None, res=None, alpha=1.0, bm=1024, bn=1024, bk=4096, name="mm"):
    m, kdim = x.shape
    n = w.shape[1]
    bm, bn, bk = _pick(m, bm), _pick(n, bn), _pick(kdim, bk)
    nk = kdim // bk
    in_specs = [pl.BlockSpec((bm, bk), lambda i, j, k: (i, k)),
                pl.BlockSpec((bk, bn), lambda i, j, k: (k, j))]
    args = [x, w]
    if res is not None:
        in_specs.append(pl.BlockSpec((bm, bn), lambda i, j, k: (i, j)))
        args.append(res)
    return pl.pallas_call(
        functools.partial(_mm_kernel, nk=nk, act=act, alpha=alpha, has_res=res is not None),
        grid=(m // bm, n // bn, nk),
        in_specs=in_specs,
        out_specs=pl.BlockSpec((bm, bn), lambda i, j, k: (i, j)),
        out_shape=jax.ShapeDtypeStruct((m, n), out_dtype),
        scratch_shapes=[pltpu.VMEM((bm, bn), F32)] if nk > 1 else [],
        compiler_params=_cparams(("parallel", "parallel", "arbitrary")),
        name=name,
    )(*args)


def _ln_kernel(h_ref, g_ref, b_ref, o_ref, obf_ref):
    h = h_ref[...]
    mu = jnp.mean(h, axis=-1, keepdims=True)
    d = h - mu
    var = jnp.mean(d * d, axis=-1, keepdims=True)
    y = d * lax.rsqrt(var + LN_EPS) * g_ref[...] + b_ref[...]
    o_ref[...] = y
    obf_ref[...] = y.astype(BF16)


def _layer_norm(h, g, b, *, bm=256):
    m, d = h.shape
    bm = _pick(m, bm)
    row = pl.BlockSpec((bm, d), lambda i: (i, 0))
    vec = pl.BlockSpec((1, d), lambda i: (0, 0))
    return pl.pallas_call(
        _ln_kernel,
        grid=(m // bm,),
        in_specs=[row, vec, vec],
        out_specs=[row, row],
        out_shape=[jax.ShapeDtypeStruct((m, d), F32), jax.ShapeDtypeStruct((m, d), BF16)],
        compiler_params=_cparams(("parallel",)),
        name="layer_norm",
    )(h, g.reshape(1, d), b.reshape(1, d))


def _rope_tables(pos):
    half = ROT_DIM // 2
    inv = ROPE_THETA ** (-jnp.arange(half, dtype=F32) / half)
    ang = pos.astype(F32)[:, None] * inv[None, :]
    cos, sin = jnp.cos(ang), jnp.sin(ang)
    n = pos.shape[0]
    one = jnp.ones((n, HEAD_DIM - ROT_DIM), F32)
    zero = jnp.zeros((n, HEAD_DIM - ROT_DIM), F32)
    zh = jnp.zeros((n, half), F32)
    c = jnp.concatenate([cos, cos, one], 1)
    s1 = jnp.concatenate([zh, sin, zero], 1)
    s2 = jnp.concatenate([-sin, zh, zero], 1)
    return c, s1, s2


def _rope(x, c, s1, s2):
    return x * c + pltpu.roll(x, ROT_DIM // 2, 1) * s1 + pltpu.roll(x, HEAD_DIM - ROT_DIM // 2, 1) * s2


def _split_proj(proj_ref, cd, ad):
    offs = [0, cd, 2 * cd, 3 * cd, 3 * cd + ad]
    for _ in range(5):
        offs.append(offs[-1] + KV_DIM)
    return offs


def _prep_prompt_kernel(proj_ref, c_ref, s1_ref, s2_ref, cw_ref,
                        yconv_ref, q_ref, cmp_ref, sel_ref, win_ref, selbf_ref, winbf_ref, nconv_ref,
                        carry_ref, *, cd, ad, tt):
    t = pl.program_id(1)
    c, s1, s2 = c_ref[...], s1_ref[...], s2_ref[...]
    offs = _split_proj(proj_ref, cd, ad)

    @pl.when(t == 0)
    def _():
        carry_ref[...] = jnp.zeros_like(carry_ref)

    gb = proj_ref[:, offs[0]:offs[0] + cd]
    u = proj_ref[:, offs[1]:offs[1] + cd] * proj_ref[:, offs[2]:offs[2] + cd]
    row = lax.broadcasted_iota(jnp.int32, (tt, cd), 0)
    c0 = carry_ref[0:1, :]
    c1 = carry_ref[1:2, :]
    u1 = jnp.where(row == 0, c1, pltpu.roll(u, 1, 0))
    u2 = jnp.where(row == 0, c0, jnp.where(row == 1, c1, pltpu.roll(u, 2, 0)))
    y = cw_ref[0:1, :] * u2 + cw_ref[1:2, :] * u1 + cw_ref[2:3, :] * u
    yconv_ref[...] = (gb * y).astype(BF16)
    tail = (proj_ref[tt - 2:tt, offs[1]:offs[1] + cd] * proj_ref[tt - 2:tt, offs[2]:offs[2] + cd])
    carry_ref[...] = tail
    nconv_ref[...] = tail

    scale = HEAD_DIM ** -0.5
    for h in range(ad // HEAD_DIM):
        lo = offs[3] + h * HEAD_DIM
        q_ref[:, h * HEAD_DIM:(h + 1) * HEAD_DIM] = (_rope(proj_ref[:, lo:lo + HEAD_DIM], c, s1, s2) * scale).astype(BF16)

    for k_off, out_ref, bf_ref in ((offs[4], cmp_ref, None), (offs[6], sel_ref, selbf_ref), (offs[8], win_ref, winbf_ref)):
        for h in range(N_KV):
            lo = k_off + h * HEAD_DIM
            kr = _rope(proj_ref[:, lo:lo + HEAD_DIM], c, s1, s2)
            v = proj_ref[:, lo + KV_DIM:lo + KV_DIM + HEAD_DIM]
            out_ref[:, h * HEAD_DIM:(h + 1) * HEAD_DIM] = kr
            out_ref[:, KV_DIM + h * HEAD_DIM:KV_DIM + (h + 1) * HEAD_DIM] = v
            if bf_ref is not None:
                bf_ref[:, h * HEAD_DIM:(h + 1) * HEAD_DIM] = kr.astype(BF16)
                bf_ref[:, KV_DIM + h * HEAD_DIM:KV_DIM + (h + 1) * HEAD_DIM] = v.astype(BF16)


def _prep_prompt(proj, tables, conv_w, *, bsz, t_len, cd, ad, tt=256):
    m, nmain = proj.shape
    tt = _pick(t_len, tt)
    nt = t_len // tt
    rows = lambda w: pl.BlockSpec((tt, w), lambda b, t: (b * nt + t, 0))
    tab = pl.BlockSpec((tt, HEAD_DIM), lambda b, t: (t, 0))
    kv2 = 2 * KV_DIM
    return pl.pallas_call(
        functools.partial(_prep_prompt_kernel, cd=cd, ad=ad, tt=tt),
        grid=(bsz, nt),
        in_specs=[rows(nmain), tab, tab, tab, pl.BlockSpec((CONV_W, cd), lambda b, t: (0, 0))],
        out_specs=[rows(cd), rows(ad), rows(kv2), rows(kv2), rows(kv2), rows(kv2), rows(kv2),
                   pl.BlockSpec((None, CONV_W - 1, cd), lambda b, t: (b, 0, 0))],
        out_shape=[jax.ShapeDtypeStruct((m, cd), BF16), jax.ShapeDtypeStruct((m, ad), BF16),
                   jax.ShapeDtypeStruct((m, kv2), F32), jax.ShapeDtypeStruct((m, kv2), F32),
                   jax.ShapeDtypeStruct((m, kv2), F32), jax.ShapeDtypeStruct((m, kv2), BF16),
                   jax.ShapeDtypeStruct((m, kv2), BF16), jax.ShapeDtypeStruct((bsz, CONV_W - 1, cd), F32)],
        scratch_shapes=[pltpu.VMEM((CONV_W - 1, cd), F32)],
        compiler_params=_cparams(("parallel", "arbitrary")),
        name="prep_prompt",
    )(proj, *tables, conv_w)


def _prep_sample_kernel(proj_ref, c_ref, s1_ref, s2_ref, cw_ref, b0_ref, b1_ref,
                        yconv_ref, q_ref, cmp_ref, sel_ref, win_ref, u_ref, *, cd, ad):
    c, s1, s2 = c_ref[...], s1_ref[...], s2_ref[...]
    offs = _split_proj(proj_ref, cd, ad)
    gb = proj_ref[:, offs[0]:offs[0] + cd]
    u = proj_ref[:, offs[1]:offs[1] + cd] * proj_ref[:, offs[2]:offs[2] + cd]
    y = cw_ref[0:1, :] * b0_ref[...] + cw_ref[1:2, :] * b1_ref[...] + cw_ref[2:3, :] * u
    yconv_ref[...] = gb * y
    u_ref[...] = u
    scale = HEAD_DIM ** -0.5
    for h in range(ad // HEAD_DIM):
        lo = offs[3] + h * HEAD_DIM
        q_ref[:, h * HEAD_DIM:(h + 1) * HEAD_DIM] = (_rope(proj_ref[:, lo:lo + HEAD_DIM], c, s1, s2) * scale).astype(BF16)
    for k_off, out_ref in ((offs[4], cmp_ref), (offs[6], sel_ref), (offs[8], win_ref)):
        for h in range(N_KV):
            lo = k_off + h * HEAD_DIM
            out_ref[:, h * HEAD_DIM:(h + 1) * HEAD_DIM] = _rope(proj_ref[:, lo:lo + HEAD_DIM], c, s1, s2)
            out_ref[:, KV_DIM + h * HEAD_DIM:KV_DIM + (h + 1) * HEAD_DIM] = proj_ref[:, lo + KV_DIM:lo + KV_DIM + HEAD_DIM]


def _prep_sample(proj, tables, conv_w, buf0, buf1, *, cd, ad):
    db = proj.shape[0]
    kv2 = 2 * KV_DIM
    full = lambda a: pl.BlockSpec(a.shape, lambda i: (0,) * a.ndim)
    ins = [proj, *tables, conv_w, buf0, buf1]
    outs = [jax.ShapeDtypeStruct((db, cd), F32), jax.ShapeDtypeStruct((db, ad), BF16),
            jax.ShapeDtypeStruct((db, kv2), F32), jax.ShapeDtypeStruct((db, kv2), F32),
            jax.ShapeDtypeStruct((db, kv2), F32), jax.ShapeDtypeStruct((db, cd), F32)]
    return pl.pallas_call(
        functools.partial(_prep_sample_kernel, cd=cd, ad=ad),
        grid=(1,),
        in_specs=[full(a) for a in ins],
        out_specs=[full(o) for o in outs],
        out_shape=outs,
        compiler_params=_cparams(("arbitrary",)),
        name="prep_sample",
    )(*ins)


def _cmp12_kernel(pt_ref, *refs, npg):
    del pt_ref
    pages, w_ref, o_ref = refs[:npg], refs[npg], refs[npg + 1]
    sub = lax.broadcasted_iota(jnp.int32, (4, 2 * N_KV, HEAD_DIM), 1)
    lhs_k, lhs_v = [], []
    for l in range(STRIDE_CMP):
        ks, vs = [], []
        for pr in pages:
            x = pr[:, l, :, :]
            a, b = x[0:4], x[4:8]
            k = jnp.where(sub < N_KV, a, pltpu.roll(b, N_KV, 1))
            v = jnp.where(sub < N_KV, pltpu.roll(a, N_KV, 1), b)
            ks.append(k.reshape(4 * 2 * N_KV, HEAD_DIM))
            vs.append(v.reshape(4 * 2 * N_KV, HEAD_DIM))
        lhs_k.append(jnp.concatenate(ks, 0) if npg > 1 else ks[0])
        lhs_v.append(jnp.concatenate(vs, 0) if npg > 1 else vs[0])
    lk = jnp.concatenate(lhs_k, 1).astype(BF16)
    lv = jnp.concatenate(lhs_v, 1).astype(BF16)
    o_ref[0] = jnp.dot(lk, w_ref[0], preferred_element_type=F32)
    o_ref[1] = jnp.dot(lv, w_ref[1], preferred_element_type=F32)


def _compress_chunks(pages, page_table, w12, *, npg=8):
    bsz, n_pages = page_table.shape
    page_rows = pages.shape[1]
    cpp = page_rows // STRIDE_CMP
    assert cpp == 8, "compression kernel regroups exactly 8 chunks per page"
    npg = _pick(n_pages, npg)
    ngrp = n_pages // npg
    pages5 = pages.reshape(pages.shape[0], cpp, STRIDE_CMP, 2 * N_KV, HEAD_DIM)
    rows = npg * 4 * 2 * N_KV

    def page_spec(i):
        return pl.BlockSpec((None, cpp, STRIDE_CMP, 2 * N_KV, HEAD_DIM),
                            lambda b, g, pt: (pt[b * n_pages + g * npg + i], 0, 0, 0, 0))

    out = pl.pallas_call(
        functools.partial(_cmp12_kernel, npg=npg),
        grid_spec=pltpu.PrefetchScalarGridSpec(
            num_scalar_prefetch=1,
            grid=(bsz, ngrp),
            in_specs=[page_spec(i) for i in range(npg)]
            + [pl.BlockSpec(w12.shape, lambda b, g, pt: (0, 0, 0))],
            out_specs=pl.BlockSpec((None, None, 2, rows, 2 * HEAD_DIM), lambda b, g, pt: (b, g, 0, 0, 0)),
        ),
        out_shape=jax.ShapeDtypeStruct((bsz, ngrp, 2, rows, 2 * HEAD_DIM), F32),
        compiler_params=_cparams(("parallel", "parallel")),
        name="compress_chunks",
    )(page_table.reshape(-1), *([pages5] * npg), w12)
    out = out.reshape(bsz, ngrp, 2, npg, 4, 2, N_KV, 2 * HEAD_DIM)
    out = out.transpose(0, 2, 6, 1, 3, 5, 4, 7)
    return out.reshape(bsz, 2, N_KV, n_pages * cpp, 2 * HEAD_DIM)


def _cmp_finish_kernel(p_ref, pe_ref, w1_ref, w2_ref, o_ref, *, nc):
    bias = jnp.dot(pe_ref[...], w1_ref[...], preferred_element_type=F32)[0:1]
    for h in range(N_KV):
        p = p_ref[h]
        pre = p[:, :HEAD_DIM] + pltpu.roll(p[:, HEAD_DIM:], nc - 1, 0) + bias
        act = jax.nn.gelu(pre)
        o_ref[h] = jnp.dot(act.astype(BF16), w2_ref[...], preferred_element_type=F32).astype(o_ref.dtype)


def _compress_finish(p12, pe_flat, w1_flat, w2):
    bsz, _, _, nc, _ = p12.shape
    return pl.pallas_call(
        functools.partial(_cmp_finish_kernel, nc=nc),
        grid=(bsz, 2),
        in_specs=[pl.BlockSpec((None, None, N_KV, nc, 2 * HEAD_DIM), lambda b, j: (b, j, 0, 0, 0)),
                  pl.BlockSpec((None, 8, BLK_CMP * HEAD_DIM), lambda b, j: (j, 0, 0)),
                  pl.BlockSpec((None, BLK_CMP * HEAD_DIM, HEAD_DIM), lambda b, j: (j, 0, 0)),
                  pl.BlockSpec((None, HEAD_DIM, HEAD_DIM), lambda b, j: (j, 0, 0))],
        out_specs=pl.BlockSpec((None, None, N_KV, nc, HEAD_DIM), lambda b, j: (b, j, 0, 0, 0)),
        out_shape=jax.ShapeDtypeStruct((bsz, 2, N_KV, nc, HEAD_DIM), BF16),
        compiler_params=_cparams(("parallel", "parallel")),
        name="compress_finish",
    )(p12, pe_flat, w1_flat, w2)


def _dot_nt(a, b):
    return lax.dot_general(a, b, (((1,), (1,)), ((), ())), preferred_element_type=F32)


def _softmax_rows(s, mask):
    s = jnp.where(mask, s, NEG)
    m = jnp.max(s, axis=-1, keepdims=True)
    e = jnp.where(mask, jnp.exp(s - m), 0.0)
    return e / jnp.maximum(jnp.sum(e, axis=-1, keepdims=True), 1e-30)


def _dot_exact01(p, onehot_bf16):
    hi = p.astype(BF16)
    r1 = p - hi.astype(F32)
    mid = r1.astype(BF16)
    lo = (r1 - mid.astype(F32)).astype(BF16)
    out = jnp.dot(hi, onehot_bf16, preferred_element_type=F32)
    out = out + jnp.dot(mid, onehot_bf16, preferred_element_type=F32)
    return out + jnp.dot(lo, onehot_bf16, preferred_element_type=F32)


def _overlap(n_rows, n_cols, n_sel):
    n = lax.broadcasted_iota(jnp.int32, (n_rows, n_cols), 0) * STRIDE_CMP
    j = lax.broadcasted_iota(jnp.int32, (n_rows, n_cols), 1)
    ov = (n < j * BLK_SEL + BLK_SEL) & (n + BLK_CMP > j * BLK_SEL) & (j < n_sel)
    return ov.astype(BF16)


def _select_scores(imp, qpos, n_sel):
    j = lax.broadcasted_iota(jnp.int32, imp.shape, imp.ndim - 1)
    jq = qpos // BLK_SEL
    valid = j * BLK_SEL <= qpos
    forced = (j == 0) | (j == jq) | (j == jq - 1)
    score = jnp.where(valid, jnp.where(forced, FORCE, imp), -FORCE)
    return jnp.where(j < n_sel, score, -jnp.inf)


def _rank_counts(score, n_sel):
    j = lax.broadcasted_iota(jnp.int32, score.shape, score.ndim - 1)
    cnt = jnp.zeros(score.shape, jnp.int32)
    for i in range(n_sel):
        si = score[:, i:i + 1]
        beats = (si > score) | ((si == score) & (i < j))
        cnt = cnt + beats.astype(jnp.int32)
    return cnt


def _attn_prompt_kernel(q_ref, kc_ref, vc_ref, ks_ref, vs_ref, kw_ref, vw_ref, gate_ref, o_ref,
                        m_ref, l_ref, acc_ref, *, tq, tk, t_len, gqa, n_cmp, n_sel):
    i = pl.program_id(2)
    q0 = i * tq
    q = q_ref[...]
    qs = jnp.concatenate([q[:, g * HEAD_DIM:(g + 1) * HEAD_DIM] for g in range(gqa)], axis=0)
    qpos1 = q0 + lax.broadcasted_iota(jnp.int32, (tq, 1), 0)
    qpos = jnp.concatenate([qpos1] * gqa, axis=0)
    nc = kc_ref.shape[0]

    s = _dot_nt(qs, kc_ref[...])
    n = lax.broadcasted_iota(jnp.int32, (1, nc), 1)
    p_cmp = _softmax_rows(s, (n * STRIDE_CMP + BLK_CMP - 1 <= qpos) & (n < n_cmp))
    o_cmp = jnp.dot(p_cmp.astype(BF16), vc_ref[...], preferred_element_type=F32)

    psum = p_cmp[0:tq]
    for g in range(1, gqa):
        psum = psum + p_cmp[g * tq:(g + 1) * tq]
    nsl = 128 * pl.cdiv(n_sel, 128)
    imp = _dot_exact01(psum, _overlap(nc, nsl, n_sel))
    score = _select_scores(imp, qpos1, n_sel)
    chosen = (_rank_counts(score, n_sel) < N_SEL).astype(BF16)

    m_ref[...] = jnp.full(m_ref.shape, NEG, F32)
    l_ref[...] = jnp.zeros(l_ref.shape, F32)
    acc_ref[...] = jnp.zeros(acc_ref.shape, F32)

    def sel_tile(kj, carry):
        k0 = pl.multiple_of(kj * tk, tk)
        kpos = k0 + lax.broadcasted_iota(jnp.int32, (1, tk), 1)
        jrow = lax.broadcasted_iota(jnp.int32, (nsl, tk), 0)
        expand = (jrow == (k0 + lax.broadcasted_iota(jnp.int32, (nsl, tk), 1)) // BLK_SEL).astype(BF16)
        in_sel = jnp.dot(chosen, expand, preferred_element_type=F32) > 0.5
        mask1 = in_sel & (kpos <= qpos1)
        mask = jnp.concatenate([mask1] * gqa, axis=0)
        s = jnp.where(mask, _dot_nt(qs, ks_ref[pl.ds(k0, tk), :]), NEG)
        m_old = m_ref[...]
        m_new = jnp.maximum(m_old, jnp.max(s, axis=-1, keepdims=True))
        a = jnp.exp(m_old - m_new)
        p = jnp.where(mask, jnp.exp(s - m_new), 0.0)
        l_ref[...] = a * l_ref[...] + jnp.sum(p, axis=-1, keepdims=True)
        acc_ref[...] = a * acc_ref[...] + jnp.dot(p.astype(BF16), vs_ref[pl.ds(k0, tk), :], preferred_element_type=F32)
        m_ref[...] = m_new
        return carry

    lax.fori_loop(0, (q0 + tq + tk - 1) // tk, sel_tile, 0)
    o_sel = acc_ref[...] / jnp.maximum(l_ref[...], 1e-30)

    band = min(WINDOW + tq, t_len)
    kstart = pl.multiple_of(jnp.clip(q0 - WINDOW, 0, t_len - band), 8)
    kpos = kstart + lax.broadcasted_iota(jnp.int32, (1, band), 1)
    dist = qpos - kpos
    p_win = _softmax_rows(_dot_nt(qs, kw_ref[pl.ds(kstart, band), :]), (dist >= 0) & (dist < WINDOW))
    o_win = jnp.dot(p_win.astype(BF16), vw_ref[pl.ds(kstart, band), :], preferred_element_type=F32)

    gates = jax.nn.sigmoid(gate_ref[...])
    for g in range(gqa):
        rows = slice(g * tq, (g + 1) * tq)
        out = (gates[:, 3 * g:3 * g + 1] * o_cmp[rows] + gates[:, 3 * g + 1:3 * g + 2] * o_sel[rows]
               + gates[:, 3 * g + 2:3 * g + 3] * o_win[rows])
        o_ref[:, g * HEAD_DIM:(g + 1) * HEAD_DIM] = out.astype(o_ref.dtype)


def _attn_prompt(q, kv_cmp, sel_bf, win_bf, gates, *, bsz, t_len, gqa, tq=256, tk=512):
    m, ad = q.shape
    tq, tk = _pick(t_len, tq), _pick(t_len, tk)
    nt = t_len // tq
    nc = kv_cmp.shape[3]
    n_cmp = (t_len - BLK_CMP) // STRIDE_CMP + 1
    n_sel = -(-t_len // BLK_SEL)
    gw = gqa * HEAD_DIM
    qspec = pl.BlockSpec((tq, gw), lambda b, h, i: (b * nt + i, h))
    cmp_spec = lambda j: pl.BlockSpec((None, None, None, nc, HEAD_DIM), lambda b, h, i: (b, j, h, 0, 0))
    kspec = pl.BlockSpec((t_len, HEAD_DIM), lambda b, h, i: (b, h))
    vspec = pl.BlockSpec((t_len, HEAD_DIM), lambda b, h, i: (b, N_KV + h))
    return pl.pallas_call(
        functools.partial(_attn_prompt_kernel, tq=tq, tk=tk, t_len=t_len, gqa=gqa, n_cmp=n_cmp, n_sel=n_sel),
        grid=(bsz, N_KV, nt),
        in_specs=[qspec, cmp_spec(0), cmp_spec(1), kspec, vspec, kspec, vspec,
                  pl.BlockSpec((tq, HEAD_DIM), lambda b, h, i: (b * nt + i, h))],
        out_specs=qspec,
        out_shape=jax.ShapeDtypeStruct((m, ad), BF16),
        scratch_shapes=[pltpu.VMEM((gqa * tq, 1), F32), pltpu.VMEM((gqa * tq, 1), F32),
                        pltpu.VMEM((gqa * tq, HEAD_DIM), F32)],
        compiler_params=_cparams(("parallel", "parallel", "arbitrary")),
        name="attn_prompt",
    )(q, kv_cmp, kv_cmp, sel_bf, sel_bf, win_bf, win_bf, gates)


def _attn_sample_a_kernel(q_ref, kc_ref, vc_ref, kw_ref, vw_ref, nw_ref, ocmp_ref, owin_ref, idx_ref,
                          *, gqa, n_cmp, n_sel, qpos, wb):
    h = pl.program_id(1)
    qs = q_ref[...]
    nc = kc_ref.shape[0]

    s = _dot_nt(qs, kc_ref[...])
    n = lax.broadcasted_iota(jnp.int32, (1, nc), 1)
    p_cmp = _softmax_rows(s, (n * STRIDE_CMP + BLK_CMP - 1 <= qpos) & (n < n_cmp))
    ocmp_ref[...] = jnp.dot(p_cmp.astype(BF16), vc_ref[...], preferred_element_type=F32)

    psum = jnp.sum(p_cmp, axis=0, keepdims=True)
    nsl = 128 * pl.cdiv(n_sel, 128)
    imp = _dot_exact01(jnp.broadcast_to(psum, (8, nc)), _overlap(nc, nsl, n_sel))[0:1]
    score = _select_scores(imp, jnp.full((1, 1), qpos, jnp.int32), n_sel)
    cnt = _rank_counts(score, n_sel)
    lane = lax.broadcasted_iota(jnp.int32, (1, nsl), 1)
    slot = lax.broadcasted_iota(jnp.int32, (N_SEL, HEAD_DIM), 0)
    idx = jnp.zeros((N_SEL, HEAD_DIM), jnp.int32)
    for k in range(N_SEL):
        blk = jnp.sum(jnp.where(cnt == k, lane, 0), axis=-1, keepdims=True)
        idx = jnp.where(slot == k, blk, idx)
    idx_ref[...] = idx

    kn = nw_ref[pl.ds(h, 1), :]
    vn = nw_ref[pl.ds(N_KV + h, 1), :]
    sb = _dot_nt(qs, kw_ref[...].astype(BF16))
    r = lax.broadcasted_iota(jnp.int32, (1, wb), 1)
    kp = qpos - wb + r
    okb = (qpos - kp < WINDOW) & (kp >= 0)
    sb = jnp.where(okb, sb, NEG)
    sn = jnp.sum(qs.astype(F32) * kn, axis=-1, keepdims=True)
    mx = jnp.maximum(jnp.max(sb, axis=-1, keepdims=True), sn)
    pb = jnp.where(okb, jnp.exp(sb - mx), 0.0)
    pn = jnp.exp(sn - mx)
    den = jnp.sum(pb, axis=-1, keepdims=True) + pn
    num = jnp.dot(pb.astype(BF16), vw_ref[...].astype(BF16), preferred_element_type=F32) + pn * vn
    owin_ref[...] = num / den


def _attn_sample_a(q4, kv_cmp, win_state, new_win8, *, gqa, qpos):
    db = q4.shape[0]
    nc = kv_cmp.shape[3]
    wb = win_state.shape[1]
    n_cmp = (qpos + 1 - BLK_CMP) // STRIDE_CMP + 1
    n_sel = -(-(qpos + 1) // BLK_SEL)
    head = pl.BlockSpec((None, None, gqa, HEAD_DIM), lambda b, h: (b, h, 0, 0))
    cmp_spec = lambda j: pl.BlockSpec((None, None, None, nc, HEAD_DIM), lambda b, h: (b, j, h, 0, 0))
    return pl.pallas_call(
        functools.partial(_attn_sample_a_kernel, gqa=gqa, n_cmp=n_cmp, n_sel=n_sel, qpos=qpos, wb=wb),
        grid=(db, N_KV),
        in_specs=[head, cmp_spec(0), cmp_spec(1),
                  pl.BlockSpec((None, wb, HEAD_DIM), lambda b, h: (b, 0, h)),
                  pl.BlockSpec((None, wb, HEAD_DIM), lambda b, h: (b, 0, N_KV + h)),
                  pl.BlockSpec((None, 2 * N_KV, HEAD_DIM), lambda b, h: (b, 0, 0))],
        out_specs=[head, head, pl.BlockSpec((None, None, N_SEL, HEAD_DIM), lambda b, h: (b, h, 0, 0))],
        out_shape=[jax.ShapeDtypeStruct((db, N_KV, gqa, HEAD_DIM), F32),
                   jax.ShapeDtypeStruct((db, N_KV, gqa, HEAD_DIM), F32),
                   jax.ShapeDtypeStruct((db, N_KV, N_SEL, HEAD_DIM), jnp.int32)],
        compiler_params=_cparams(("parallel", "parallel")),
        name="attn_sample_cmp_win",
    )(q4, kv_cmp, kv_cmp, win_state, win_state, new_win8)


def _attn_sample_b_kernel(idx_ref, pt_ref, q_ref, k_ref, v_ref, ns_ref, ocmp_ref, owin_ref, gate_ref, o_ref,
                          m_ref, l_ref, acc_ref, *, gqa, qpos, n_past_blocks):
    del pt_ref
    b, h, k = pl.program_id(0), pl.program_id(1), pl.program_id(2)
    qs = q_ref[...]

    @pl.when(k == 0)
    def _():
        kn = ns_ref[pl.ds(h, 1), :]
        vn = ns_ref[pl.ds(N_KV + h, 1), :]
        m_ref[...] = jnp.sum(qs.astype(F32) * kn, axis=-1, keepdims=True)
        l_ref[...] = jnp.ones(l_ref.shape, F32)
        acc_ref[...] = jnp.broadcast_to(vn, acc_ref.shape)

    blk = idx_ref[(b * N_KV + h) * N_SEL + k]
    kpos = blk * BLK_SEL + lax.broadcasted_iota(jnp.int32, (1, BLK_SEL), 1)
    mask = (kpos <= qpos) & (blk < n_past_blocks)
    s = jnp.where(mask, _dot_nt(qs, k_ref[...].astype(BF16)), NEG)
    m_old = m_ref[...]
    m_new = jnp.maximum(m_old, jnp.max(s, axis=-1, keepdims=True))
    a = jnp.exp(m_old - m_new)
    p = jnp.where(mask, jnp.exp(s - m_new), 0.0)
    l_ref[...] = a * l_ref[...] + jnp.sum(p, axis=-1, keepdims=True)
    acc_ref[...] = a * acc_ref[...] + jnp.dot(p.astype(BF16), v_ref[...].astype(BF16), preferred_element_type=F32)
    m_ref[...] = m_new

    @pl.when(k == N_SEL - 1)
    def _():
        o_sel = acc_ref[...] / l_ref[...]
        gates = jax.nn.sigmoid(gate_ref[...])
        o_cmp, o_win = ocmp_ref[...], owin_ref[...]
        rows = []
        for g in range(gqa):
            rows.append(gates[:, 3 * g:3 * g + 1] * o_cmp[g:g + 1] + gates[:, 3 * g + 1:3 * g + 2] * o_sel[g:g + 1]
                        + gates[:, 3 * g + 2:3 * g + 3] * o_win[g:g + 1])
        o_ref[...] = jnp.concatenate(rows, axis=0)


def _attn_sample_b(idx, page_table, q4, sel_pages, new_sel8, o_cmp, o_win, gates4, *, gqa, qpos):
    db, n_pages = page_table.shape
    page_rows = sel_pages.shape[1]
    bpp = page_rows // BLK_SEL
    blocks = sel_pages.reshape(sel_pages.shape[0] * bpp, BLK_SEL, 2 * KV_DIM)
    n_past_blocks = n_pages * bpp

    def blk_index(b, h, k, idx_ref, pt_ref):
        blk = jnp.minimum(idx_ref[(b * N_KV + h) * N_SEL + k], n_past_blocks - 1)
        return pt_ref[b * n_pages + blk // bpp] * bpp + blk % bpp

    head = pl.BlockSpec((None, None, gqa, HEAD_DIM), lambda b, h, k, i, p: (b, h, 0, 0))
    return pl.pallas_call(
        functools.partial(_attn_sample_b_kernel, gqa=gqa, qpos=qpos, n_past_blocks=n_past_blocks),
        grid_spec=pltpu.PrefetchScalarGridSpec(
            num_scalar_prefetch=2,
            grid=(db, N_KV, N_SEL),
            in_specs=[head,
                      pl.BlockSpec((None, BLK_SEL, HEAD_DIM), lambda b, h, k, i, p: (blk_index(b, h, k, i, p), 0, h)),
                      pl.BlockSpec((None, BLK_SEL, HEAD_DIM), lambda b, h, k, i, p: (blk_index(b, h, k, i, p), 0, N_KV + h)),
                      pl.BlockSpec((None, 2 * N_KV, HEAD_DIM), lambda b, h, k, i, p: (b, 0, 0)),
                      head, head,
                      pl.BlockSpec((None, None, 1, HEAD_DIM), lambda b, h, k, i, p: (b, h, 0, 0))],
            out_specs=head,
            scratch_shapes=[pltpu.VMEM((gqa, 1), F32), pltpu.VMEM((gqa, 1), F32), pltpu.VMEM((gqa, HEAD_DIM), F32)],
        ),
        out_shape=jax.ShapeDtypeStruct((db, N_KV, gqa, HEAD_DIM), F32),
        compiler_params=_cparams(("parallel", "parallel", "arbitrary")),
        name="attn_sample_sel",
    )(idx.reshape(-1), page_table.reshape(-1), q4, blocks, blocks, new_sel8, o_cmp, o_win, gates4)


def _roll_window_kernel(s_ref, n_ref, o_ref, *, wb):
    row = lax.broadcasted_iota(jnp.int32, s_ref.shape, 0)
    o_ref[...] = jnp.where(row == wb - 1, n_ref[...], pltpu.roll(s_ref[...], wb - 1, 0))


def _roll_window(state, new_row):
    db, wb, w = state.shape
    return pl.pallas_call(
        functools.partial(_roll_window_kernel, wb=wb),
        grid=(db,),
        in_specs=[pl.BlockSpec((None, wb, w), lambda b: (b, 0, 0)), pl.BlockSpec((None, 1, w), lambda b: (b, 0, 0))],
        out_specs=pl.BlockSpec((None, wb, w), lambda b: (b, 0, 0)),
        out_shape=jax.ShapeDtypeStruct(state.shape, state.dtype),
        compiler_params=_cparams(("parallel",)),
        name="roll_window",
    )(state, new_row.reshape(db, 1, w))


def _layer_weights(w_in, cmp_pe, cmp_w1, cmp_w2, w_out, w_up, w_down, *, cd, ad, gqa):
    nmain = 3 * cd + ad + 6 * KV_DIM
    w_main = w_in[:, :nmain].astype(BF16)
    wg = w_in[:, nmain:].reshape(w_in.shape[0], N_KV, 3 * gqa)
    wg = jnp.pad(wg, ((0, 0), (0, 0), (0, HEAD_DIM - 3 * gqa))).reshape(w_in.shape[0], N_KV * HEAD_DIM).astype(BF16)
    half = STRIDE_CMP * HEAD_DIM
    w1_flat = cmp_w1.reshape(2, BLK_CMP * HEAD_DIM, HEAD_DIM).astype(BF16)
    w12 = jnp.concatenate([w1_flat[:, :half], w1_flat[:, half:]], axis=2)
    pe_flat = jnp.broadcast_to(cmp_pe.reshape(2, 1, BLK_CMP * HEAD_DIM), (2, 8, BLK_CMP * HEAD_DIM)).astype(BF16)
    return dict(w_main=w_main, w_gate=wg, w12=w12, w1_flat=w1_flat, pe_flat=pe_flat, w2=cmp_w2.astype(BF16),
                w_out=w_out.astype(BF16), w_up=w_up.astype(BF16), w_down=w_down.astype(BF16))


def _mix_and_mlp(x, x_bf_unused, mix_in, wts, ln1_g, ln1_b, ln2_g, ln2_b, *, alpha):
    h1 = _matmul(mix_in, wts["w_out"], out_dtype=F32, res=x, alpha=alpha, name="mm_out")
    x1, x1_bf = _layer_norm(h1, ln1_g, ln1_b)
    up = _matmul(x1_bf, wts["w_up"], out_dtype=BF16, act="relu2", name="mm_up")
    h2 = _matmul(up, wts["w_down"], out_dtype=F32, res=x1, alpha=alpha, bk=2048, name="mm_down")
    return _layer_norm(h2, ln2_g, ln2_b)


def _pad_rows(a, rows):
    return jnp.pad(a, ((0, rows - a.shape[0]), (0, 0)))


def kernel(x_prompt, x_sample, cache_cmp, cache_sel, state_win, state_conv, page_table, w_in, conv_w, cmp_pe, cmp_w1, cmp_w2, w_out, ln1_g, ln1_b, w_up, w_down, ln2_g, ln2_b):
    bsz, t_len, d = x_prompt.shape
    db, dec_seq, _ = x_sample.shape
    depth = w_in.shape[0]
    cd = state_conv.shape[-1]
    ad = d - cd
    gqa = ad // HEAD_DIM // N_KV
    n_pages, page_rows = page_table.shape[1], cache_cmp.shape[2]
    past = n_pages * page_rows
    wb = state_win.shape[2]
    assert dec_seq == 1 and wb == WINDOW and t_len >= WINDOW and 3 * gqa <= HEAD_DIM
    assert page_rows % BLK_SEL == 0 and past % BLK_SEL == 0
    alpha = (2.0 * depth) ** 0.25
    kv2 = 2 * KV_DIM
    spad = 16

    tab_p = _rope_tables(jnp.arange(t_len, dtype=jnp.int32))
    tab_s = _rope_tables(jnp.full((db,), past, jnp.int32))
    pt_prompt = jnp.arange(bsz * (t_len // page_rows), dtype=jnp.int32).reshape(bsz, t_len // page_rows)

    xp = x_prompt.reshape(bsz * t_len, d)
    xp_bf = xp.astype(BF16)
    xs = _pad_rows(x_sample.reshape(db, d), spad)
    xs_bf = xs.astype(BF16)

    outs = {k: [] for k in ("cmp_p", "sel_p", "win_p", "conv_p", "cmp_s", "sel_s", "win_s", "conv_s")}
    for l in range(depth):
        wts = _layer_weights(w_in[l], cmp_pe[l], cmp_w1[l], cmp_w2[l], w_out[l], w_up[l], w_down[l], cd=cd, ad=ad, gqa=gqa)

        proj = _matmul(xp_bf, wts["w_main"], out_dtype=F32, name="mm_in")
        gates = _matmul(xp_bf, wts["w_gate"], out_dtype=F32, bn=N_KV * HEAD_DIM, name="mm_gate")
        yconv, q, new_cmp, new_sel, new_win, sel_bf, win_bf, new_conv = _prep_prompt(
            proj, tab_p, conv_w[l], bsz=bsz, t_len=t_len, cd=cd, ad=ad)
        p12 = _compress_chunks(new_cmp.reshape(-1, page_rows, kv2), pt_prompt, wts["w12"])
        kv_cmp = _compress_finish(p12, wts["pe_flat"], wts["w1_flat"], wts["w2"])
        o_attn = _attn_prompt(q, kv_cmp, sel_bf, win_bf, gates, bsz=bsz, t_len=t_len, gqa=gqa)
        xp, xp_bf = _mix_and_mlp(xp, xp_bf, jnp.concatenate([yconv, o_attn], axis=1), wts,
                                 ln1_g[l], ln1_b[l], ln2_g[l], ln2_b[l], alpha=alpha)
        outs["cmp_p"].append(new_cmp.reshape(bsz, t_len, 2, N_KV, HEAD_DIM))
        outs["sel_p"].append(new_sel.reshape(bsz, t_len, 2, N_KV, HEAD_DIM))
        outs["win_p"].append(new_win.reshape(bsz, t_len, 2, N_KV, HEAD_DIM)[:, t_len - min(WINDOW, t_len):])
        outs["conv_p"].append(new_conv)

        proj_s = _matmul(xs_bf, wts["w_main"], out_dtype=F32, name="mm_in_s")[:db]
        gates_s = _matmul(xs_bf, wts["w_gate"], out_dtype=F32, bn=N_KV * HEAD_DIM, name="mm_gate_s")[:db]
        yconv_s, q_s, cmp_row, sel_row, win_row, u_s = _prep_sample(
            proj_s, tab_s, conv_w[l], state_conv[l, :, 0], state_conv[l, :, 1], cd=cd, ad=ad)
        p12_s = _compress_chunks(cache_cmp[l].reshape(-1, page_rows, kv2), page_table, wts["w12"])
        kv_cmp_s = _compress_finish(p12_s, wts["pe_flat"], wts["w1_flat"], wts["w2"])
        q4 = q_s.reshape(db, N_KV, gqa, HEAD_DIM)
        win_state = state_win[l].reshape(db, wb, kv2)
        o_cmp_s, o_win_s, idx = _attn_sample_a(q4, kv_cmp_s, win_state, win_row.reshape(db, 2 * N_KV, HEAD_DIM),
                                               gqa=gqa, qpos=past)
        o_attn_s = _attn_sample_b(idx[..., 0], page_table, q4, cache_sel[l].reshape(-1, page_rows, kv2),
                                  sel_row.reshape(db, 2 * N_KV, HEAD_DIM), o_cmp_s, o_win_s,
                                  gates_s.reshape(db, N_KV, 1, HEAD_DIM), gqa=gqa, qpos=past)
        mix_s = _pad_rows(jnp.concatenate([yconv_s, o_attn_s.reshape(db, ad)], axis=1), spad).astype(BF16)
        xs, xs_bf = _mix_and_mlp(xs, xs_bf, mix_s, wts, ln1_g[l], ln1_b[l], ln2_g[l], ln2_b[l], alpha=alpha)
        outs["cmp_s"].append(cmp_row.reshape(db, 1, 2, N_KV, HEAD_DIM))
        outs["sel_s"].append(sel_row.reshape(db, 1, 2, N_KV, HEAD_DIM))
        outs["win_s"].append(_roll_window(win_state, win_row).reshape(db, wb, 2, N_KV, HEAD_DIM))
        outs["conv_s"].append(jnp.stack([state_conv[l, :, 1], u_s], axis=1))

    st = {k: jnp.stack(v) for k, v in outs.items()}
    return (xp.reshape(bsz, t_len, d), xs[:db].reshape(db, 1, d), st["cmp_p"], st["sel_p"], st["win_p"], st["conv_p"],
            st["cmp_s"], st["sel_s"], st["win_s"], st["conv_s"])
```

```python
import functools
import math

import jax
import jax.numpy as jnp
from jax import lax
from jax.experimental import pallas as pl
from jax.experimental.pallas import tpu as pltpu

F32 = jnp.float32
BF16 = jnp.bfloat16

HEAD_DIM = 128
N_KV = 4
KV_DIM = N_KV * HEAD_DIM
ROT_DIM = HEAD_DIM // 4
ROPE_THETA = 500000.0
BLK_CMP = 32
STRIDE_CMP = 16
BLK_SEL = 64
N_SEL = 16
WINDOW = 512
CONV_W = 3
LN_EPS = 1e-5
FORCE = 1e4
NEG = -1e30
Q_SCALE = HEAD_DIM ** -0.5 * math.log2(math.e)
SAMPLE_ROWS = 16

VMEM_LIMIT_BYTES = 56 * 1024 * 1024


def _cparams(sem):
    return pltpu.CompilerParams(dimension_semantics=sem, vmem_limit_bytes=VMEM_LIMIT_BYTES)


def _pick(n, pref):
    if n <= pref:
        return n
    t = pref
    while n % t:
        t //= 2
    return t


def _epilogue(acc, res, *, act, alpha):
    if act == "relu2":
        acc = jnp.square(jnp.maximum(acc, 0.0))
    if res is not None:
        acc = alpha * res + acc
    return acc


def _mm_ws_kernel(*refs, n_x, ksplit, act, alpha, has_res, w_is_nk):
    w_ref = refs[0]
    x_refs = refs[1:1 + n_x]
    xs_refs = refs[1 + n_x:1 + 2 * n_x]
    pos = 1 + 2 * n_x
    res_ref, ress_ref = (refs[pos], refs[pos + 1]) if has_res else (None, None)
    pos += 2 * has_res
    o_ref, os_ref, wbf_ref = refs[pos], refs[pos + 1], refs[pos + 2]

    def product(x_list):
        acc = None
        for x_ref, (k0, k1) in zip(x_list, ksplit):
            part = jnp.dot(x_ref[...], wbf_ref[k0:k1, :], preferred_element_type=F32)
            acc = part if acc is None else acc + part
        return acc

    @pl.when(pl.program_id(1) == 0)
    def _():
        w = w_ref[...]
        wbf_ref[...] = (w.T if w_is_nk else w).astype(BF16)
        res_s = ress_ref[...] if has_res else None
        os_ref[...] = _epilogue(product(xs_refs), res_s, act=act, alpha=alpha).astype(os_ref.dtype)

    res = res_ref[...] if has_res else None
    o_ref[...] = _epilogue(product(x_refs), res, act=act, alpha=alpha).astype(o_ref.dtype)


def _matmul_ws(xs, xs_s, w, layer, *, n_out, out_dtype, act=None, res=None, res_s=None, alpha=1.0,
               w_is_nk=False, bm=1024, bn=512, name="mm"):
    m = xs[0].shape[0]
    ms = xs_s[0].shape[0]
    kdim = w.shape[2] if w_is_nk else w.shape[1]
    bm, bn = _pick(m, bm), _pick(n_out, bn)
    ksplit, k0 = [], 0
    for x in xs:
        ksplit.append((k0, k0 + x.shape[1]))
        k0 += x.shape[1]
    assert k0 == kdim
    if w_is_nk:
        in_specs = [pl.BlockSpec((None, bn, kdim), lambda j, i: (layer, j, 0))]
    else:
        in_specs = [pl.BlockSpec((None, kdim, bn), lambda j, i: (layer, 0, j))]
    in_specs += [pl.BlockSpec((bm, x.shape[1]), lambda j, i: (i, 0)) for x in xs]
    in_specs += [pl.BlockSpec((ms, x.shape[1]), lambda j, i: (0, 0)) for x in xs_s]
    args = [w, *xs, *xs_s]
    if res is not None:
        in_specs += [pl.BlockSpec((bm, bn), lambda j, i: (i, j)), pl.BlockSpec((ms, bn), lambda j, i: (0, j))]
        args += [res, res_s]
    return pl.pallas_call(
        functools.partial(_mm_ws_kernel, n_x=len(xs), ksplit=tuple(ksplit), act=act, alpha=alpha,
                          has_res=res is not None, w_is_nk=w_is_nk),
        grid=(n_out // bn, m // bm),
        in_specs=in_specs,
        out_specs=[pl.BlockSpec((bm, bn), lambda j, i: (i, j)), pl.BlockSpec((ms, bn), lambda j, i: (0, j))],
        out_shape=[jax.ShapeDtypeStruct((m, n_out), out_dtype), jax.ShapeDtypeStruct((ms, n_out), out_dtype)],
        scratch_shapes=[pltpu.VMEM((kdim, bn), BF16)],
        compiler_params=_cparams(("arbitrary", "arbitrary")),
        name=name,
    )(*args)


def _mm_kernel(*refs, nk, act, alpha, has_res):
    x_ref, w_ref = refs[0], refs[1]
    res_ref = refs[2] if has_res else None
    o_ref = refs[2 + has_res]
    acc_ref = refs[3 + has_res] if nk > 1 else None
    part = jnp.dot(x_ref[...], w_ref[...], preferred_element_type=F32)

    def finish(acc):
        res = res_ref[...] if has_res else None
        o_ref[...] = _epilogue(acc, res, act=act, alpha=alpha).astype(o_ref.dtype)

    if nk == 1:
        finish(part)
    else:
        k = pl.program_id(2)

        @pl.when(k == 0)
        def _():
            acc_ref[...] = part

        @pl.when(k > 0)
        def _():
            acc_ref[...] += part

        @pl.when(k == nk - 1)
        def _():
            finish(acc_ref[...])


def _matmul(x, w, layer=None, *, out_dtype, act=None, res=None, alpha=1.0, bm=1024, bn=1024, bk=4096, name="mm"):
    m, kdim = x.shape
    n = w.shape[-1]
    bm, bn, bk = _pick(m, bm), _pick(n, bn), _pick(kdim, bk)
    nk = kdim // bk
    if layer is None:
        w_spec = pl.BlockSpec((bk, bn), lambda i, j, k: (k, j))
    else:
        w_spec = pl.BlockSpec((None, bk, bn), lambda i, j, k: (layer, k, j))
    in_specs = [pl.BlockSpec((bm, bk), lambda i, j, k: (i, k)), w_spec]
    args = [x, w]
    if res is not None:
        in_specs.append(pl.BlockSpec((bm, bn), lambda i, j, k: (i, j)))
        args.append(res)
    return pl.pallas_call(
        functools.partial(_mm_kernel, nk=nk, act=act, alpha=alpha, has_res=res is not None),
        grid=(m // bm, n // bn, nk),
        in_specs=in_specs,
        out_specs=pl.BlockSpec((bm, bn), lambda i, j, k: (i, j)),
        out_shape=jax.ShapeDtypeStruct((m, n), out_dtype),
        scratch_shapes=[pltpu.VMEM((bm, bn), F32)] if nk > 1 else [],
        compiler_params=_cparams(("parallel", "parallel", "arbitrary")),
        name=name,
    )(*args)


def _ln_kernel(h_ref, g_ref, b_ref, o_ref, obf_ref):
    h = h_ref[...]
    mu = jnp.mean(h, axis=-1, keepdims=True)
    d = h - mu
    var = jnp.mean(d * d, axis=-1, keepdims=True)
    y = d * lax.rsqrt(var + LN_EPS) * g_ref[...] + b_ref[...]
    o_ref[...] = y
    obf_ref[...] = y.astype(BF16)


def _layer_norm(h, g, b, layer, *, bm=256):
    m, d = h.shape
    bm = _pick(m, bm)
    row = pl.BlockSpec((bm, d), lambda i: (i, 0))
    vec = pl.BlockSpec((None, 1, d), lambda i: (layer, 0, 0))
    return pl.pallas_call(
        _ln_kernel,
        grid=(m // bm,),
        in_specs=[row, vec, vec],
        out_specs=[row, row],
        out_shape=[jax.ShapeDtypeStruct((m, d), F32), jax.ShapeDtypeStruct((m, d), BF16)],
        compiler_params=_cparams(("parallel",)),
        name="layer_norm",
    )(h, g, b)


def _rope_tables(pos):
    half = ROT_DIM // 2
    inv = ROPE_THETA ** (-jnp.arange(half, dtype=F32) / half)
    ang = pos.astype(F32)[:, None] * inv[None, :]
    cos, sin = jnp.cos(ang), jnp.sin(ang)
    n = pos.shape[0]
    one = jnp.ones((n, HEAD_DIM - ROT_DIM), F32)
    zero = jnp.zeros((n, HEAD_DIM - ROT_DIM), F32)
    zh = jnp.zeros((n, half), F32)
    c = jnp.concatenate([cos, cos, one], 1)
    s1 = jnp.concatenate([zh, sin, zero], 1)
    s2 = jnp.concatenate([-sin, zh, zero], 1)
    return c, s1, s2


def _rope(x, c, s1, s2):
    return x * c + pltpu.roll(x, ROT_DIM // 2, 1) * s1 + pltpu.roll(x, HEAD_DIM - ROT_DIM // 2, 1) * s2


def _proj_offsets(cd, ad):
    offs = [0, cd, 2 * cd, 3 * cd, 3 * cd + ad]
    for _ in range(5):
        offs.append(offs[-1] + KV_DIM)
    return offs


def _prep_prompt_kernel(proj_ref, c_ref, s1_ref, s2_ref, cw_ref,
                        yconv_ref, q_ref, cmp_ref, sel_ref, win_ref, selbf_ref, winbf_ref, nconv_ref,
                        carry_ref, *, cd, ad, tt):
    t = pl.program_id(1)
    c, s1, s2 = c_ref[...], s1_ref[...], s2_ref[...]
    offs = _proj_offsets(cd, ad)

    @pl.when(t == 0)
    def _():
        carry_ref[...] = jnp.zeros_like(carry_ref)

    gb = proj_ref[:, offs[0]:offs[0] + cd]
    u = proj_ref[:, offs[1]:offs[1] + cd] * proj_ref[:, offs[2]:offs[2] + cd]
    row = lax.broadcasted_iota(jnp.int32, (tt, cd), 0)
    c0 = carry_ref[0:1, :]
    c1 = carry_ref[1:2, :]
    u1 = jnp.where(row == 0, c1, pltpu.roll(u, 1, 0))
    u2 = jnp.where(row == 0, c0, jnp.where(row == 1, c1, pltpu.roll(u, 2, 0)))
    y = cw_ref[0:1, :] * u2 + cw_ref[1:2, :] * u1 + cw_ref[2:3, :] * u
    yconv_ref[...] = (gb * y).astype(BF16)
    tail = (proj_ref[tt - 2:tt, offs[1]:offs[1] + cd] * proj_ref[tt - 2:tt, offs[2]:offs[2] + cd])
    carry_ref[...] = tail
    nconv_ref[...] = tail

    for h in range(ad // HEAD_DIM):
        lo = offs[3] + h * HEAD_DIM
        q_ref[:, h * HEAD_DIM:(h + 1) * HEAD_DIM] = (_rope(proj_ref[:, lo:lo + HEAD_DIM], c, s1, s2) * Q_SCALE).astype(BF16)

    for k_off, out_ref, bf_ref in ((offs[4], cmp_ref, None), (offs[6], sel_ref, selbf_ref), (offs[8], win_ref, winbf_ref)):
        for h in range(N_KV):
            lo = k_off + h * HEAD_DIM
            kr = _rope(proj_ref[:, lo:lo + HEAD_DIM], c, s1, s2)
            v = proj_ref[:, lo + KV_DIM:lo + KV_DIM + HEAD_DIM]
            out_ref[:, h, :] = kr
            out_ref[:, N_KV + h, :] = v
            if bf_ref is not None:
                bf_ref[:, h * HEAD_DIM:(h + 1) * HEAD_DIM] = kr.astype(BF16)
                bf_ref[:, KV_DIM + h * HEAD_DIM:KV_DIM + (h + 1) * HEAD_DIM] = v.astype(BF16)


def _prep_prompt(proj, tables, conv_w, layer, *, bsz, t_len, cd, ad, tt=256):
    m, nmain = proj.shape
    tt = _pick(t_len, tt)
    nt = t_len // tt
    rows = lambda w: pl.BlockSpec((tt, w), lambda b, t: (b * nt + t, 0))
    rows3 = pl.BlockSpec((tt, 2 * N_KV, HEAD_DIM), lambda b, t: (b * nt + t, 0, 0))
    tab = pl.BlockSpec((tt, HEAD_DIM), lambda b, t: (t, 0))
    kv2 = 2 * KV_DIM
    kv_rows = jax.ShapeDtypeStruct((m, 2 * N_KV, HEAD_DIM), F32)
    return pl.pallas_call(
        functools.partial(_prep_prompt_kernel, cd=cd, ad=ad, tt=tt),
        grid=(bsz, nt),
        in_specs=[rows(nmain), tab, tab, tab, pl.BlockSpec((None, CONV_W, cd), lambda b, t: (layer, 0, 0))],
        out_specs=[rows(cd), rows(ad), rows3, rows3, rows3, rows(kv2), rows(kv2),
                   pl.BlockSpec((None, CONV_W - 1, cd), lambda b, t: (b, 0, 0))],
        out_shape=[jax.ShapeDtypeStruct((m, cd), BF16), jax.ShapeDtypeStruct((m, ad), BF16),
                   kv_rows, kv_rows, kv_rows, jax.ShapeDtypeStruct((m, kv2), BF16),
                   jax.ShapeDtypeStruct((m, kv2), BF16), jax.ShapeDtypeStruct((bsz, CONV_W - 1, cd), F32)],
        scratch_shapes=[pltpu.VMEM((CONV_W - 1, cd), F32)],
        compiler_params=_cparams(("parallel", "arbitrary")),
        name="prep_prompt",
    )(proj, *tables, conv_w)


def _prep_sample_kernel(proj_ref, c_ref, s1_ref, s2_ref, cw_ref, b0_ref, b1_ref,
                        yconv_ref, q_ref, cmp_ref, sel_ref, win_ref, u_ref, *, cd, ad):
    c, s1, s2 = c_ref[...], s1_ref[...], s2_ref[...]
    offs = _proj_offsets(cd, ad)
    gb = proj_ref[:, offs[0]:offs[0] + cd]
    u = proj_ref[:, offs[1]:offs[1] + cd] * proj_ref[:, offs[2]:offs[2] + cd]
    y = cw_ref[0:1, :] * b0_ref[...] + cw_ref[1:2, :] * b1_ref[...] + cw_ref[2:3, :] * u
    yconv_ref[...] = gb * y
    u_ref[...] = u
    for h in range(ad // HEAD_DIM):
        lo = offs[3] + h * HEAD_DIM
        q_ref[:, h * HEAD_DIM:(h + 1) * HEAD_DIM] = (_rope(proj_ref[:, lo:lo + HEAD_DIM], c, s1, s2) * Q_SCALE).astype(BF16)
    for k_off, out_ref in ((offs[4], cmp_ref), (offs[6], sel_ref), (offs[8], win_ref)):
        for h in range(N_KV):
            lo = k_off + h * HEAD_DIM
            out_ref[:, h * HEAD_DIM:(h + 1) * HEAD_DIM] = _rope(proj_ref[:, lo:lo + HEAD_DIM], c, s1, s2)
            out_ref[:, KV_DIM + h * HEAD_DIM:KV_DIM + (h + 1) * HEAD_DIM] = proj_ref[:, lo + KV_DIM:lo + KV_DIM + HEAD_DIM]


def _prep_sample(proj, tables, conv_w, buf0, buf1, *, cd, ad):
    db = proj.shape[0]
    kv2 = 2 * KV_DIM
    full = lambda a: pl.BlockSpec(a.shape, lambda i: (0,) * a.ndim)
    ins = [proj, *tables, conv_w, buf0, buf1]
    outs = [jax.ShapeDtypeStruct((db, cd), F32), jax.ShapeDtypeStruct((db, ad), BF16),
            jax.ShapeDtypeStruct((db, kv2), F32), jax.ShapeDtypeStruct((db, kv2), F32),
            jax.ShapeDtypeStruct((db, kv2), F32), jax.ShapeDtypeStruct((db, cd), F32)]
    return pl.pallas_call(
        functools.partial(_prep_sample_kernel, cd=cd, ad=ad),
        grid=(1,),
        in_specs=[full(a) for a in ins],
        out_specs=[full(o) for o in outs],
        out_shape=outs,
        compiler_params=_cparams(("arbitrary",)),
        name="prep_sample",
    )(*ins)


def _cmp12_kernel(pt_ref, *refs, npg):
    del pt_ref
    pages, w_ref, o_ref = refs[:npg], refs[npg], refs[npg + 1]
    cpp = pages[0].shape[0]
    lhs = ([], [])
    for l in range(STRIDE_CMP):
        per_page = [pltpu.einshape("cjd->jcd", pr[:, l, :, :]) for pr in pages]
        for j in range(2):
            lhs[j].append(jnp.concatenate([xt[j * N_KV + h] for h in range(N_KV) for xt in per_page], axis=0))
    for j in range(2):
        rows = jnp.concatenate(lhs[j], axis=1).astype(BF16)
        out = jnp.dot(rows, w_ref[j], preferred_element_type=F32)
        o_ref[j] = out.reshape(N_KV, npg * cpp, 2 * HEAD_DIM)


def _compress_chunks(pages5, page_ids, w12, layer, *, npg=8):
    bsz, n_pages = page_ids.shape
    cpp = pages5.shape[1]
    assert cpp == 8 and pages5.shape[2] == STRIDE_CMP, "one (8, 128) tile holds a row of all chunks of a page"
    npg = _pick(n_pages, npg)

    def page_spec(i):
        return pl.BlockSpec((None, cpp, STRIDE_CMP, 2 * N_KV, HEAD_DIM),
                            lambda b, g, pt: (pt[b * n_pages + g * npg + i], 0, 0, 0, 0))

    return pl.pallas_call(
        functools.partial(_cmp12_kernel, npg=npg),
        grid_spec=pltpu.PrefetchScalarGridSpec(
            num_scalar_prefetch=1,
            grid=(bsz, n_pages // npg),
            in_specs=[page_spec(i) for i in range(npg)]
            + [pl.BlockSpec((None,) + w12.shape[1:], lambda b, g, pt: (layer, 0, 0, 0))],
            out_specs=pl.BlockSpec((None, 2, N_KV, npg * cpp, 2 * HEAD_DIM), lambda b, g, pt: (b, 0, 0, g, 0)),
        ),
        out_shape=jax.ShapeDtypeStruct((bsz, 2, N_KV, n_pages * cpp, 2 * HEAD_DIM), F32),
        compiler_params=_cparams(("parallel", "parallel")),
        name="compress_chunks",
    )(page_ids.reshape(-1), *([pages5] * npg), w12)


def _cmp_finish_kernel(p_ref, pe_ref, w1_ref, w2_ref, o_ref, *, nc):
    bias = jnp.dot(pe_ref[...], w1_ref[...], preferred_element_type=F32)[0:1]
    for h in range(N_KV):
        p = p_ref[h]
        pre = p[:, :HEAD_DIM] + pltpu.roll(p[:, HEAD_DIM:], nc - 1, 0) + bias
        act = jax.nn.gelu(pre)
        o_ref[h] = jnp.dot(act.astype(BF16), w2_ref[...], preferred_element_type=F32).astype(o_ref.dtype)


def _compress_finish(p12, pe_flat, w1_flat, w2, layer):
    bsz, _, _, nc, _ = p12.shape
    return pl.pallas_call(
        functools.partial(_cmp_finish_kernel, nc=nc),
        grid=(bsz, 2),
        in_specs=[pl.BlockSpec((None, None, N_KV, nc, 2 * HEAD_DIM), lambda b, j: (b, j, 0, 0, 0)),
                  pl.BlockSpec((None, None, 8, BLK_CMP * HEAD_DIM), lambda b, j: (layer, j, 0, 0)),
                  pl.BlockSpec((None, None, BLK_CMP * HEAD_DIM, HEAD_DIM), lambda b, j: (layer, j, 0, 0)),
                  pl.BlockSpec((None, None, HEAD_DIM, HEAD_DIM), lambda b, j: (layer, j, 0, 0))],
        out_specs=pl.BlockSpec((None, None, N_KV, nc, HEAD_DIM), lambda b, j: (b, j, 0, 0, 0)),
        out_shape=jax.ShapeDtypeStruct((bsz, 2, N_KV, nc, HEAD_DIM), BF16),
        compiler_params=_cparams(("parallel", "parallel")),
        name="compress_finish",
    )(p12, pe_flat, w1_flat, w2)


def _dot_nt(a, b):
    return lax.dot_general(a, b, (((1,), (1,)), ((), ())), preferred_element_type=F32)


def _dot_exact01(p, onehot_bf16):
    hi = p.astype(BF16)
    r1 = p - hi.astype(F32)
    mid = r1.astype(BF16)
    lo = (r1 - mid.astype(F32)).astype(BF16)
    out = jnp.dot(hi, onehot_bf16, preferred_element_type=F32)
    out = out + jnp.dot(mid, onehot_bf16, preferred_element_type=F32)
    return out + jnp.dot(lo, onehot_bf16, preferred_element_type=F32)


def _overlap(n_rows, n_cols, n_sel):
    n = lax.broadcasted_iota(jnp.int32, (n_rows, n_cols), 0) * STRIDE_CMP
    j = lax.broadcasted_iota(jnp.int32, (n_rows, n_cols), 1)
    ov = (n < j * BLK_SEL + BLK_SEL) & (n + BLK_CMP > j * BLK_SEL) & (j < n_sel)
    return ov.astype(BF16)


def _select_scores(imp, qpos, n_sel):
    j = lax.broadcasted_iota(jnp.int32, imp.shape, imp.ndim - 1)
    jq = qpos // BLK_SEL
    valid = j * BLK_SEL <= qpos
    forced = (j == 0) | (j == jq) | (j == jq - 1)
    score = jnp.where(valid, jnp.where(forced, FORCE, imp), -FORCE)
    return jnp.where(j < n_sel, score, -jnp.inf)


def _rank_counts(score, n_sel):
    j = lax.broadcasted_iota(jnp.int32, score.shape, score.ndim - 1)
    cnt = jnp.zeros(score.shape, jnp.int32)
    for i in range(n_sel):
        si = score[:, i:i + 1]
        beats = (si > score) | ((si == score) & (i < j))
        cnt = cnt + beats.astype(jnp.int32)
    return cnt


def _attn_prompt_kernel(q_ref, kc_ref, vc_ref, ks_ref, vs_ref, kw_ref, vw_ref, gate_ref, o_ref,
                        m_ref, l_ref, acc_ref, *, tq, tk, t_len, gqa, n_cmp, n_sel):
    i = pl.program_id(2)
    q0 = i * tq
    q = q_ref[...]
    qs = jnp.concatenate([q[:, g * HEAD_DIM:(g + 1) * HEAD_DIM] for g in range(gqa)], axis=0)
    qpos = q0 + lax.broadcasted_iota(jnp.int32, (tq, 1), 0)
    nc = kc_ref.shape[0]
    rows = gqa * tq

    def heads(x):
        return x.reshape(gqa, tq, x.shape[-1])

    n = lax.broadcasted_iota(jnp.int32, (1, nc), 1)
    bias_c = jnp.where((n * STRIDE_CMP + BLK_CMP - 1 <= qpos) & (n < n_cmp), 0.0, NEG)
    s = heads(_dot_nt(qs, kc_ref[...])) + bias_c[None]
    e = jnp.exp2(s - jnp.max(s, axis=-1, keepdims=True))
    has_block = (qpos >= BLK_CMP - 1).astype(F32)[None]
    p_cmp = e * (has_block / jnp.sum(e, axis=-1, keepdims=True))
    o_cmp = heads(jnp.dot(p_cmp.reshape(rows, nc).astype(BF16), vc_ref[...], preferred_element_type=F32))

    nsl = 128 * pl.cdiv(n_sel, 128)
    jlane = lax.broadcasted_iota(jnp.int32, (tq, nsl), 1)

    def all_started():
        return (jlane * BLK_SEL <= qpos).astype(BF16)

    def top_k():
        imp = _dot_exact01(jnp.sum(p_cmp, axis=0), _overlap(nc, nsl, n_sel))
        score = _select_scores(imp, qpos, n_sel)
        return (_rank_counts(score, n_sel) < N_SEL).astype(BF16)

    chosen = lax.cond((q0 + tq - 1) // BLK_SEL + 1 <= N_SEL, all_started, top_k)

    m_ref[...] = jnp.full(m_ref.shape, NEG, F32)
    l_ref[...] = jnp.zeros(l_ref.shape, F32)
    acc_ref[...] = jnp.zeros(acc_ref.shape, F32)

    def sel_tile(kj, carry):
        k0 = pl.multiple_of(kj * tk, tk)
        kpos = k0 + lax.broadcasted_iota(jnp.int32, (1, tk), 1)
        jrow = lax.broadcasted_iota(jnp.int32, (nsl, tk), 0)
        expand = (jrow == (k0 + lax.broadcasted_iota(jnp.int32, (nsl, tk), 1)) // BLK_SEL).astype(BF16)
        in_sel = jnp.dot(chosen, expand, preferred_element_type=F32) > 0.5
        bias = jnp.where(in_sel & (kpos <= qpos), 0.0, NEG)
        s = heads(_dot_nt(qs, ks_ref[pl.ds(k0, tk), :])) + bias[None]
        m_old = heads(m_ref[...])
        m_new = jnp.maximum(m_old, jnp.max(s, axis=-1, keepdims=True))
        a = jnp.exp2(m_old - m_new)
        p = jnp.exp2(s - m_new)
        l_ref[...] = (a * heads(l_ref[...]) + jnp.sum(p, axis=-1, keepdims=True)).reshape(rows, 1)
        pv = jnp.dot(p.reshape(rows, tk).astype(BF16), vs_ref[pl.ds(k0, tk), :], preferred_element_type=F32)
        acc_ref[...] = (a * heads(acc_ref[...]) + heads(pv)).reshape(rows, HEAD_DIM)
        m_ref[...] = m_new.reshape(rows, 1)
        return carry

    lax.fori_loop(0, (q0 + tq + tk - 1) // tk, sel_tile, 0)
    o_sel = heads(acc_ref[...] / l_ref[...])

    band = min(WINDOW + tq, t_len)
    kstart = pl.multiple_of(jnp.clip(q0 - WINDOW, 0, t_len - band), 8)
    dist = qpos - (kstart + lax.broadcasted_iota(jnp.int32, (1, band), 1))
    bias_w = jnp.where((dist >= 0) & (dist < WINDOW), 0.0, NEG)
    s = heads(_dot_nt(qs, kw_ref[pl.ds(kstart, band), :])) + bias_w[None]
    e = jnp.exp2(s - jnp.max(s, axis=-1, keepdims=True))
    pv = jnp.dot(e.reshape(rows, band).astype(BF16), vw_ref[pl.ds(kstart, band), :], preferred_element_type=F32)
    o_win = heads(pv) / jnp.sum(e, axis=-1, keepdims=True)

    gates = jax.nn.sigmoid(gate_ref[...])
    for g in range(gqa):
        out = (gates[:, 3 * g:3 * g + 1] * o_cmp[g] + gates[:, 3 * g + 1:3 * g + 2] * o_sel[g]
               + gates[:, 3 * g + 2:3 * g + 3] * o_win[g])
        o_ref[:, g * HEAD_DIM:(g + 1) * HEAD_DIM] = out.astype(o_ref.dtype)


def _attn_prompt(q, kv_cmp, sel_bf, win_bf, gates, *, bsz, t_len, gqa, tq=256, tk=512):
    m, ad = q.shape
    tq, tk = _pick(t_len, tq), _pick(t_len, tk)
    nt = t_len // tq
    nc = kv_cmp.shape[3]
    n_cmp = (t_len - BLK_CMP) // STRIDE_CMP + 1
    n_sel = -(-t_len // BLK_SEL)
    gw = gqa * HEAD_DIM
    qspec = pl.BlockSpec((tq, gw), lambda b, h, i: (b * nt + i, h))
    cmp_spec = lambda j: pl.BlockSpec((None, None, None, nc, HEAD_DIM), lambda b, h, i: (b, j, h, 0, 0))
    kspec = pl.BlockSpec((t_len, HEAD_DIM), lambda b, h, i: (b, h))
    vspec = pl.BlockSpec((t_len, HEAD_DIM), lambda b, h, i: (b, N_KV + h))
    return pl.pallas_call(
        functools.partial(_attn_prompt_kernel, tq=tq, tk=tk, t_len=t_len, gqa=gqa, n_cmp=n_cmp, n_sel=n_sel),
        grid=(bsz, N_KV, nt),
        in_specs=[qspec, cmp_spec(0), cmp_spec(1), kspec, vspec, kspec, vspec,
                  pl.BlockSpec((tq, HEAD_DIM), lambda b, h, i: (b * nt + i, h))],
        out_specs=qspec,
        out_shape=jax.ShapeDtypeStruct((m, ad), BF16),
        scratch_shapes=[pltpu.VMEM((gqa * tq, 1), F32), pltpu.VMEM((gqa * tq, 1), F32),
                        pltpu.VMEM((gqa * tq, HEAD_DIM), F32)],
        compiler_params=_cparams(("parallel", "parallel", "arbitrary")),
        name="attn_prompt",
    )(q, kv_cmp, kv_cmp, sel_bf, sel_bf, win_bf, win_bf, gates)


def _attn_sample_a_kernel(q_ref, kc_ref, vc_ref, win_ref, nw_ref, ocmp_ref, owin_ref, idx_ref,
                          *, gqa, n_cmp, n_sel, qpos, wb):
    h = pl.program_id(1)
    qs = q_ref[...]
    nc = kc_ref.shape[0]

    n = lax.broadcasted_iota(jnp.int32, (1, nc), 1)
    ok = (n * STRIDE_CMP + BLK_CMP - 1 <= qpos) & (n < n_cmp)
    s = jnp.where(ok, _dot_nt(qs, kc_ref[...]), NEG)
    e = jnp.where(ok, jnp.exp2(s - jnp.max(s, axis=-1, keepdims=True)), 0.0)
    p_cmp = e / jnp.maximum(jnp.sum(e, axis=-1, keepdims=True), 1e-30)
    ocmp_ref[...] = jnp.dot(p_cmp.astype(BF16), vc_ref[...], preferred_element_type=F32)

    psum = jnp.sum(p_cmp, axis=0, keepdims=True)
    nsl = 128 * pl.cdiv(n_sel, 128)
    imp = _dot_exact01(jnp.broadcast_to(psum, (8, nc)), _overlap(nc, nsl, n_sel))[0:1]
    score = _select_scores(imp, jnp.full((1, 1), qpos, jnp.int32), n_sel)
    cnt = _rank_counts(score, n_sel)
    lane = lax.broadcasted_iota(jnp.int32, (1, nsl), 1)
    slot = lax.broadcasted_iota(jnp.int32, (N_SEL, HEAD_DIM), 0)
    idx = jnp.zeros((N_SEL, HEAD_DIM), jnp.int32)
    for k in range(N_SEL):
        blk = jnp.sum(jnp.where(cnt == k, lane, 0), axis=-1, keepdims=True)
        idx = jnp.where(slot == k, blk, idx)
    idx_ref[...] = idx

    width = 2 * N_KV
    rows = win_ref[...].reshape(wb * width, HEAD_DIM).astype(BF16)
    kn = nw_ref[pl.ds(h, 1), :]
    vn = nw_ref[pl.ds(N_KV + h, 1), :]
    col = lax.broadcasted_iota(jnp.int32, (1, wb * width), 1)
    kp = qpos - wb + col // width
    okb = (col % width == h) & (qpos - kp < WINDOW) & (kp >= 0)
    sb = jnp.where(okb, _dot_nt(qs, rows), NEG)
    sn = jnp.sum(qs.astype(F32) * kn, axis=-1, keepdims=True)
    mx = jnp.maximum(jnp.max(sb, axis=-1, keepdims=True), sn)
    pb = jnp.where(okb, jnp.exp2(sb - mx), 0.0)
    pn = jnp.exp2(sn - mx)
    den = jnp.sum(pb, axis=-1, keepdims=True) + pn
    num = jnp.dot(pltpu.roll(pb, N_KV, 1).astype(BF16), rows, preferred_element_type=F32) + pn * vn
    owin_ref[...] = num / den


def _attn_sample_a(q4, kv_cmp, win_all, new_win8, layer, *, gqa, qpos):
    db = q4.shape[0]
    nc = kv_cmp.shape[3]
    wb = win_all.shape[1]
    n_cmp = (qpos + 1 - BLK_CMP) // STRIDE_CMP + 1
    n_sel = -(-(qpos + 1) // BLK_SEL)
    head = pl.BlockSpec((None, None, gqa, HEAD_DIM), lambda b, h: (b, h, 0, 0))
    cmp_spec = lambda j: pl.BlockSpec((None, None, None, nc, HEAD_DIM), lambda b, h: (b, j, h, 0, 0))
    return pl.pallas_call(
        functools.partial(_attn_sample_a_kernel, gqa=gqa, n_cmp=n_cmp, n_sel=n_sel, qpos=qpos, wb=wb),
        grid=(db, N_KV),
        in_specs=[head, cmp_spec(0), cmp_spec(1),
                  pl.BlockSpec((None, wb, 2 * N_KV, HEAD_DIM), lambda b, h: (layer * db + b, 0, 0, 0)),
                  pl.BlockSpec((None, 2 * N_KV, HEAD_DIM), lambda b, h: (b, 0, 0))],
        out_specs=[head, head, pl.BlockSpec((None, None, N_SEL, HEAD_DIM), lambda b, h: (b, h, 0, 0))],
        out_shape=[jax.ShapeDtypeStruct((db, N_KV, gqa, HEAD_DIM), F32),
                   jax.ShapeDtypeStruct((db, N_KV, gqa, HEAD_DIM), F32),
                   jax.ShapeDtypeStruct((db, N_KV, N_SEL, HEAD_DIM), jnp.int32)],
        compiler_params=_cparams(("parallel", "parallel")),
        name="attn_sample_cmp_win",
    )(q4, kv_cmp, kv_cmp, win_all, new_win8)


def _attn_sample_b_kernel(idx_ref, phys_ref, *refs, gqa, qpos, n_past_blocks):
    del phys_ref
    q_ref = refs[0]
    blk_refs = refs[1:1 + N_SEL]
    ns_ref, ocmp_ref, owin_ref, gate_ref, o_ref = refs[1 + N_SEL:]
    b, h = pl.program_id(0), pl.program_id(1)
    qs = q_ref[...]
    width = 2 * N_KV
    ncol = BLK_SEL * width
    col = lax.broadcasted_iota(jnp.int32, (1, ncol), 1)

    kn = ns_ref[pl.ds(h, 1), :]
    vn = ns_ref[pl.ds(N_KV + h, 1), :]
    sn = jnp.sum(qs.astype(F32) * kn, axis=-1, keepdims=True)

    scores, mx = [], sn
    for k in range(N_SEL):
        blk = idx_ref[(b * N_KV + h) * N_SEL + k]
        ok = (col % width == h) & (blk * BLK_SEL + col // width <= qpos) & (blk < n_past_blocks)
        s = jnp.where(ok, _dot_nt(qs, blk_refs[k][...].reshape(ncol, HEAD_DIM).astype(BF16)), NEG)
        scores.append(s)
        mx = jnp.maximum(mx, jnp.max(s, axis=-1, keepdims=True))
    pn = jnp.exp2(sn - mx)
    den, num = pn, pn * vn
    for k in range(N_SEL):
        p = jnp.exp2(scores[k] - mx)
        den = den + jnp.sum(p, axis=-1, keepdims=True)
        num = num + jnp.dot(pltpu.roll(p, N_KV, 1).astype(BF16),
                            blk_refs[k][...].reshape(ncol, HEAD_DIM).astype(BF16), preferred_element_type=F32)
    o_sel = num / den

    gates = jax.nn.sigmoid(gate_ref[...])
    o_cmp, o_win = ocmp_ref[...], owin_ref[...]
    rows = []
    for g in range(gqa):
        rows.append(gates[:, 3 * g:3 * g + 1] * o_cmp[g:g + 1] + gates[:, 3 * g + 1:3 * g + 2] * o_sel[g:g + 1]
                    + gates[:, 3 * g + 2:3 * g + 3] * o_win[g:g + 1])
    o_ref[...] = jnp.concatenate(rows, axis=0)


def _attn_sample_b(idx, phys, q4, sel_blocks, new_sel8, o_cmp, o_win, gates4, *, gqa, qpos, n_past_blocks):
    db = q4.shape[0]
    head = pl.BlockSpec((None, None, gqa, HEAD_DIM), lambda b, h, i, p: (b, h, 0, 0))

    def blk_spec(k):
        return pl.BlockSpec((None, BLK_SEL, 2 * N_KV, HEAD_DIM),
                            lambda b, h, i, p: (p[(b * N_KV + h) * N_SEL + k], 0, 0, 0))

    return pl.pallas_call(
        functools.partial(_attn_sample_b_kernel, gqa=gqa, qpos=qpos, n_past_blocks=n_past_blocks),
        grid_spec=pltpu.PrefetchScalarGridSpec(
            num_scalar_prefetch=2,
            grid=(db, N_KV),
            in_specs=[head] + [blk_spec(k) for k in range(N_SEL)]
            + [pl.BlockSpec((None, 2 * N_KV, HEAD_DIM), lambda b, h, i, p: (b, 0, 0)),
               head, head,
               pl.BlockSpec((None, None, 1, HEAD_DIM), lambda b, h, i, p: (b, h, 0, 0))],
            out_specs=head,
        ),
        out_shape=jax.ShapeDtypeStruct((db, N_KV, gqa, HEAD_DIM), F32),
        compiler_params=_cparams(("parallel", "parallel")),
        name="attn_sample_sel",
    )(idx.reshape(-1), phys.reshape(-1), q4, *([sel_blocks] * N_SEL), new_sel8, o_cmp, o_win, gates4)


def _roll_window_kernel(s_ref, n_ref, o_ref, *, wb):
    o_ref[0:wb - 1] = s_ref[1:wb]
    o_ref[wb - 1] = n_ref[...]


def _roll_window(state, new_rows):
    n, wb = state.shape[:2]
    tile = (2 * N_KV, HEAD_DIM)
    return pl.pallas_call(
        functools.partial(_roll_window_kernel, wb=wb),
        grid=(n,),
        in_specs=[pl.BlockSpec((None, wb) + tile, lambda b: (b, 0, 0, 0)), pl.BlockSpec((None,) + tile, lambda b: (b, 0, 0))],
        out_specs=pl.BlockSpec((None, wb) + tile, lambda b: (b, 0, 0, 0)),
        out_shape=jax.ShapeDtypeStruct(state.shape, state.dtype),
        compiler_params=_cparams(("parallel",)),
        name="roll_window",
    )(state, new_rows)


def _pad_rows(a, rows):
    return jnp.pad(a, ((0, rows - a.shape[0]), (0, 0)))


def kernel(x_prompt, x_sample, cache_cmp, cache_sel, state_win, state_conv, page_table, w_in, conv_w, cmp_pe, cmp_w1, cmp_w2, w_out, ln1_g, ln1_b, w_up, w_down, ln2_g, ln2_b):
    bsz, t_len, d = x_prompt.shape
    db, dec_seq, _ = x_sample.shape
    depth = w_in.shape[0]
    cd = state_conv.shape[-1]
    ad = d - cd
    gqa = ad // HEAD_DIM // N_KV
    pool, page_rows = cache_cmp.shape[1], cache_cmp.shape[2]
    n_pages = page_table.shape[1]
    past = n_pages * page_rows
    wb = state_win.shape[2]
    assert dec_seq == 1 and wb == WINDOW and t_len >= WINDOW and 3 * gqa <= HEAD_DIM and db <= SAMPLE_ROWS
    assert page_rows % BLK_SEL == 0 and t_len % page_rows == 0
    alpha = (2.0 * depth) ** 0.25
    nmain = 3 * cd + ad + 6 * KV_DIM
    cpp = page_rows // STRIDE_CMP
    bpp = page_rows // BLK_SEL
    tile = (2 * N_KV, HEAD_DIM)

    tab_p = _rope_tables(jnp.arange(t_len, dtype=jnp.int32))
    tab_s = _rope_tables(jnp.full((db,), past, jnp.int32))
    pt_prompt = jnp.arange(bsz * (t_len // page_rows), dtype=jnp.int32).reshape(bsz, t_len // page_rows)

    wg = w_in[:, :, nmain:].reshape(depth, d, N_KV, 3 * gqa)
    wg = jnp.pad(wg, ((0, 0), (0, 0), (0, 0), (0, HEAD_DIM - 3 * gqa))).reshape(depth, d, KV_DIM).astype(BF16)
    half = STRIDE_CMP * HEAD_DIM
    w1_flat = cmp_w1.reshape(depth, 2, BLK_CMP * HEAD_DIM, HEAD_DIM).astype(BF16)
    w12 = jnp.concatenate([w1_flat[:, :, :half], w1_flat[:, :, half:]], axis=3)
    pe_flat = jnp.broadcast_to(cmp_pe.reshape(depth, 2, 1, BLK_CMP * HEAD_DIM), (depth, 2, 8, BLK_CMP * HEAD_DIM)).astype(BF16)
    w2 = cmp_w2.astype(BF16)
    w_down_bf = w_down.astype(BF16)
    w_in_nk = jnp.swapaxes(w_in, 1, 2)
    ln = [a.reshape(depth, 1, d) for a in (ln1_g, ln1_b, ln2_g, ln2_b)]

    cmp_pages = cache_cmp.reshape(depth * pool, cpp, STRIDE_CMP, *tile)
    sel_blocks = cache_sel.reshape(depth * pool * bpp, BLK_SEL, *tile)
    win_all = state_win.reshape(depth * db, wb, *tile)

    xp = x_prompt.reshape(bsz * t_len, d)
    xp_bf = xp.astype(BF16)
    xs = _pad_rows(x_sample.reshape(db, d), SAMPLE_ROWS)
    xs_bf = xs.astype(BF16)

    outs = {k: [] for k in ("cmp_p", "sel_p", "win_p", "conv_p", "cmp_s", "sel_s", "win_s", "conv_s")}
    for l in range(depth):
        proj, proj_s = _matmul_ws([xp_bf], [xs_bf], w_in_nk, l, n_out=nmain, out_dtype=F32, w_is_nk=True, name="mm_in")
        gates = _matmul(xp_bf, wg, l, out_dtype=F32, bn=KV_DIM, name="mm_gate")
        gates_s = _matmul(xs_bf, wg, l, out_dtype=F32, bn=KV_DIM, name="mm_gate_s")[:db]

        yconv, q, new_cmp, new_sel, new_win, sel_bf, win_bf, new_conv = _prep_prompt(
            proj, tab_p, conv_w, l, bsz=bsz, t_len=t_len, cd=cd, ad=ad)
        p12 = _compress_chunks(new_cmp.reshape(-1, cpp, STRIDE_CMP, *tile), pt_prompt, w12, l)
        kv_cmp = _compress_finish(p12, pe_flat, w1_flat, w2, l)
        o_attn = _attn_prompt(q, kv_cmp, sel_bf, win_bf, gates, bsz=bsz, t_len=t_len, gqa=gqa)

        yconv_s, q_s, cmp_row, sel_row, win_row, u_s = _prep_sample(
            proj_s[:db], tab_s, conv_w[l], state_conv[l, :, 0], state_conv[l, :, 1], cd=cd, ad=ad)
        p12_s = _compress_chunks(cmp_pages, page_table + l * pool, w12, l)
        kv_cmp_s = _compress_finish(p12_s, pe_flat, w1_flat, w2, l)
        q4 = q_s.reshape(db, N_KV, gqa, HEAD_DIM)
        o_cmp_s, o_win_s, idx = _attn_sample_a(q4, kv_cmp_s, win_all, win_row.reshape(db, *tile), l, gqa=gqa, qpos=past)
        idx = idx[..., 0]
        page = jnp.minimum(idx // bpp, n_pages - 1).reshape(db, N_KV * N_SEL)
        phys = (jnp.take_along_axis(page_table, page, axis=1) + l * pool) * bpp + idx.reshape(db, -1) % bpp
        o_attn_s = _attn_sample_b(idx, phys, q4, sel_blocks, sel_row.reshape(db, *tile), o_cmp_s, o_win_s,
                                  gates_s.reshape(db, N_KV, 1, HEAD_DIM), gqa=gqa, qpos=past, n_past_blocks=n_pages * bpp)
        yconv_s = _pad_rows(yconv_s, SAMPLE_ROWS).astype(BF16)
        o_attn_s = _pad_rows(o_attn_s.reshape(db, ad), SAMPLE_ROWS).astype(BF16)

        h1, h1_s = _matmul_ws([yconv, o_attn], [yconv_s, o_attn_s], w_out, l, n_out=d, out_dtype=F32,
                              res=xp, res_s=xs, alpha=alpha, name="mm_out")
        x1, x1_bf = _layer_norm(h1, ln[0], ln[1], l)
        x1_s, x1_s_bf = _layer_norm(h1_s, ln[0], ln[1], l)
        up, up_s = _matmul_ws([x1_bf], [x1_s_bf], w_up, l, n_out=w_up.shape[2], out_dtype=BF16, act="relu2", name="mm_up")
        h2 = _matmul(up, w_down_bf, l, out_dtype=F32, res=x1, alpha=alpha, bk=2048, name="mm_down")
        h2_s = _matmul(up_s, w_down_bf, l, out_dtype=F32, res=x1_s, alpha=alpha, bk=2048, name="mm_down_s")
        xp, xp_bf = _layer_norm(h2, ln[2], ln[3], l)
        xs, xs_bf = _layer_norm(h2_s, ln[2], ln[3], l)

        kv_shape = (2, N_KV, HEAD_DIM)
        outs["cmp_p"].append(new_cmp.reshape(bsz, t_len, *kv_shape))
        outs["sel_p"].append(new_sel.reshape(bsz, t_len, *kv_shape))
        outs["win_p"].append(new_win.reshape(bsz, t_len, *kv_shape)[:, t_len - min(WINDOW, t_len):])
        outs["conv_p"].append(new_conv)
        outs["cmp_s"].append(cmp_row.reshape(db, 1, *kv_shape))
        outs["sel_s"].append(sel_row.reshape(db, 1, *kv_shape))
        outs["win_s"].append(win_row.reshape(db, *tile))
        outs["conv_s"].append(jnp.stack([state_conv[l, :, 1], u_s], axis=1))

    st = {k: jnp.stack(v) for k, v in outs.items()}
    new_win_s = _roll_window(win_all, st["win_s"].reshape(depth * db, *tile)).reshape(depth, db, wb, 2, N_KV, HEAD_DIM)
    return (xp.reshape(bsz, t_len, d), xs[:db].reshape(db, 1, d), st["cmp_p"], st["sel_p"], st["win_p"], st["conv_p"],
            st["cmp_s"], st["sel_s"], new_win_s, st["conv_s"])
```

```python
import functools
import math

import jax
import jax.numpy as jnp
from jax import lax
from jax.experimental import pallas as pl
from jax.experimental.pallas import tpu as pltpu

F32 = jnp.float32
BF16 = jnp.bfloat16

HEAD_DIM = 128
N_KV = 4
KV_DIM = N_KV * HEAD_DIM
ROT_DIM = HEAD_DIM // 4
ROPE_THETA = 500000.0
BLK_CMP = 32
STRIDE_CMP = 16
BLK_SEL = 64
N_SEL = 16
WINDOW = 512
CONV_W = 3
LN_EPS = 1e-5
FORCE = 1e4
NEG = -1e30
Q_SCALE = HEAD_DIM ** -0.5 * math.log2(math.e)
SAMPLE_ROWS = 16
ATTN_TQ = 256

VMEM_LIMIT_BYTES = 56 * 1024 * 1024


def _cparams(sem):
    return pltpu.CompilerParams(dimension_semantics=sem, vmem_limit_bytes=VMEM_LIMIT_BYTES)


def _pick(n, pref):
    if n <= pref:
        return n
    t = pref
    while n % t:
        t //= 2
    return t


def _epilogue(acc, res, *, act, alpha):
    if act == "relu2":
        acc = jnp.square(jnp.maximum(acc, 0.0))
    if res is not None:
        acc = alpha * res + acc
    return acc


def _mm_ws_kernel(*refs, n_x, ksplit, act, alpha, has_res, w_is_nk):
    w_ref = refs[0]
    x_refs = refs[1:1 + n_x]
    xs_refs = refs[1 + n_x:1 + 2 * n_x]
    pos = 1 + 2 * n_x
    res_ref, ress_ref = (refs[pos], refs[pos + 1]) if has_res else (None, None)
    pos += 2 * has_res
    o_ref, os_ref, wbf_ref = refs[pos], refs[pos + 1], refs[pos + 2]

    def product(x_list):
        acc = None
        for x_ref, (k0, k1) in zip(x_list, ksplit):
            part = jnp.dot(x_ref[...], wbf_ref[k0:k1, :], preferred_element_type=F32)
            acc = part if acc is None else acc + part
        return acc

    @pl.when(pl.program_id(1) == 0)
    def _():
        w = w_ref[...]
        wbf_ref[...] = (w.T if w_is_nk else w).astype(BF16)
        res_s = ress_ref[...] if has_res else None
        os_ref[...] = _epilogue(product(xs_refs), res_s, act=act, alpha=alpha).astype(os_ref.dtype)

    res = res_ref[...] if has_res else None
    o_ref[...] = _epilogue(product(x_refs), res, act=act, alpha=alpha).astype(o_ref.dtype)


def _matmul_ws(xs, xs_s, w, layer, *, n_out, out_dtype, act=None, res=None, res_s=None, alpha=1.0,
               w_is_nk=False, bm=1024, bn=512, name="mm"):
    m = xs[0].shape[0]
    ms = xs_s[0].shape[0]
    kdim = w.shape[2] if w_is_nk else w.shape[1]
    bm, bn = _pick(m, bm), _pick(n_out, bn)
    ksplit, k0 = [], 0
    for x in xs:
        ksplit.append((k0, k0 + x.shape[1]))
        k0 += x.shape[1]
    assert k0 == kdim
    if w_is_nk:
        in_specs = [pl.BlockSpec((None, bn, kdim), lambda j, i: (layer, j, 0))]
    else:
        in_specs = [pl.BlockSpec((None, kdim, bn), lambda j, i: (layer, 0, j))]
    in_specs += [pl.BlockSpec((bm, x.shape[1]), lambda j, i: (i, 0)) for x in xs]
    in_specs += [pl.BlockSpec((ms, x.shape[1]), lambda j, i: (0, 0)) for x in xs_s]
    args = [w, *xs, *xs_s]
    if res is not None:
        in_specs += [pl.BlockSpec((bm, bn), lambda j, i: (i, j)), pl.BlockSpec((ms, bn), lambda j, i: (0, j))]
        args += [res, res_s]
    return pl.pallas_call(
        functools.partial(_mm_ws_kernel, n_x=len(xs), ksplit=tuple(ksplit), act=act, alpha=alpha,
                          has_res=res is not None, w_is_nk=w_is_nk),
        grid=(n_out // bn, m // bm),
        in_specs=in_specs,
        out_specs=[pl.BlockSpec((bm, bn), lambda j, i: (i, j)), pl.BlockSpec((ms, bn), lambda j, i: (0, j))],
        out_shape=[jax.ShapeDtypeStruct((m, n_out), out_dtype), jax.ShapeDtypeStruct((ms, n_out), out_dtype)],
        scratch_shapes=[pltpu.VMEM((kdim, bn), BF16)],
        compiler_params=_cparams(("arbitrary", "arbitrary")),
        name=name,
    )(*args)


def _mm_kernel(*refs, nk, act, alpha, has_res):
    x_ref, w_ref = refs[0], refs[1]
    res_ref = refs[2] if has_res else None
    o_ref = refs[2 + has_res]
    acc_ref = refs[3 + has_res] if nk > 1 else None
    part = jnp.dot(x_ref[...], w_ref[...], preferred_element_type=F32)

    def finish(acc):
        res = res_ref[...] if has_res else None
        o_ref[...] = _epilogue(acc, res, act=act, alpha=alpha).astype(o_ref.dtype)

    if nk == 1:
        finish(part)
    else:
        k = pl.program_id(2)

        @pl.when(k == 0)
        def _():
            acc_ref[...] = part

        @pl.when(k > 0)
        def _():
            acc_ref[...] += part

        @pl.when(k == nk - 1)
        def _():
            finish(acc_ref[...])


def _matmul(x, w, layer=None, *, out_dtype, act=None, res=None, alpha=1.0, bm=1024, bn=1024, bk=4096, name="mm"):
    m, kdim = x.shape
    n = w.shape[-1]
    bm, bn, bk = _pick(m, bm), _pick(n, bn), _pick(kdim, bk)
    nk = kdim // bk
    if layer is None:
        w_spec = pl.BlockSpec((bk, bn), lambda i, j, k: (k, j))
    else:
        w_spec = pl.BlockSpec((None, bk, bn), lambda i, j, k: (layer, k, j))
    in_specs = [pl.BlockSpec((bm, bk), lambda i, j, k: (i, k)), w_spec]
    args = [x, w]
    if res is not None:
        in_specs.append(pl.BlockSpec((bm, bn), lambda i, j, k: (i, j)))
        args.append(res)
    return pl.pallas_call(
        functools.partial(_mm_kernel, nk=nk, act=act, alpha=alpha, has_res=res is not None),
        grid=(m // bm, n // bn, nk),
        in_specs=in_specs,
        out_specs=pl.BlockSpec((bm, bn), lambda i, j, k: (i, j)),
        out_shape=jax.ShapeDtypeStruct((m, n), out_dtype),
        scratch_shapes=[pltpu.VMEM((bm, bn), F32)] if nk > 1 else [],
        compiler_params=_cparams(("parallel", "parallel", "arbitrary")),
        name=name,
    )(*args)


def _ln_kernel(h_ref, g_ref, b_ref, o_ref, obf_ref):
    h = h_ref[...]
    mu = jnp.mean(h, axis=-1, keepdims=True)
    d = h - mu
    var = jnp.mean(d * d, axis=-1, keepdims=True)
    y = d * lax.rsqrt(var + LN_EPS) * g_ref[...] + b_ref[...]
    o_ref[...] = y
    obf_ref[...] = y.astype(BF16)


def _layer_norm(h, g, b, layer, *, bm=256):
    m, d = h.shape
    bm = _pick(m, bm)
    row = pl.BlockSpec((bm, d), lambda i: (i, 0))
    vec = pl.BlockSpec((None, 1, d), lambda i: (layer, 0, 0))
    return pl.pallas_call(
        _ln_kernel,
        grid=(m // bm,),
        in_specs=[row, vec, vec],
        out_specs=[row, row],
        out_shape=[jax.ShapeDtypeStruct((m, d), F32), jax.ShapeDtypeStruct((m, d), BF16)],
        compiler_params=_cparams(("parallel",)),
        name="layer_norm",
    )(h, g, b)


def _rope_tables(pos):
    half = ROT_DIM // 2
    inv = ROPE_THETA ** (-jnp.arange(half, dtype=F32) / half)
    ang = pos.astype(F32)[:, None] * inv[None, :]
    cos, sin = jnp.cos(ang), jnp.sin(ang)
    n = pos.shape[0]
    one = jnp.ones((n, HEAD_DIM - ROT_DIM), F32)
    zero = jnp.zeros((n, HEAD_DIM - ROT_DIM), F32)
    zh = jnp.zeros((n, half), F32)
    c = jnp.concatenate([cos, cos, one], 1)
    s1 = jnp.concatenate([zh, sin, zero], 1)
    s2 = jnp.concatenate([-sin, zh, zero], 1)
    return c, s1, s2


def _rope(x, c, s1, s2):
    return x * c + pltpu.roll(x, ROT_DIM // 2, 1) * s1 + pltpu.roll(x, HEAD_DIM - ROT_DIM // 2, 1) * s2


def _proj_offsets(cd, ad):
    offs = [0, cd, 2 * cd, 3 * cd, 3 * cd + ad]
    for _ in range(5):
        offs.append(offs[-1] + KV_DIM)
    return offs


def _prep_prompt_kernel(proj_ref, c_ref, s1_ref, s2_ref, cw_ref,
                        yconv_ref, qt_ref, cmp_ref, sel_ref, win_ref, selk_ref, selvt_ref, wink_ref, winvt_ref,
                        nconv_ref, carry_ref, *, cd, ad, tt):
    t = pl.program_id(1)
    c, s1, s2 = c_ref[...], s1_ref[...], s2_ref[...]
    offs = _proj_offsets(cd, ad)

    @pl.when(t == 0)
    def _():
        carry_ref[...] = jnp.zeros_like(carry_ref)

    gb = proj_ref[:, offs[0]:offs[0] + cd]
    u = proj_ref[:, offs[1]:offs[1] + cd] * proj_ref[:, offs[2]:offs[2] + cd]
    row = lax.broadcasted_iota(jnp.int32, (tt, cd), 0)
    c0 = carry_ref[0:1, :]
    c1 = carry_ref[1:2, :]
    u1 = jnp.where(row == 0, c1, pltpu.roll(u, 1, 0))
    u2 = jnp.where(row == 0, c0, jnp.where(row == 1, c1, pltpu.roll(u, 2, 0)))
    y = cw_ref[0:1, :] * u2 + cw_ref[1:2, :] * u1 + cw_ref[2:3, :] * u
    yconv_ref[...] = (gb * y).astype(BF16)
    tail = (proj_ref[tt - 2:tt, offs[1]:offs[1] + cd] * proj_ref[tt - 2:tt, offs[2]:offs[2] + cd])
    carry_ref[...] = tail
    nconv_ref[...] = tail

    gqa = ad // HEAD_DIM // N_KV
    for h in range(ad // HEAD_DIM):
        lo = offs[3] + h * HEAD_DIM
        g = h % gqa
        qt = (_rope(proj_ref[:, lo:lo + HEAD_DIM], c, s1, s2) * Q_SCALE).T
        qt_ref[h // gqa, :, g * tt:(g + 1) * tt] = qt.astype(BF16)

    for k_off, out_ref, k_ref, vt_ref in ((offs[4], cmp_ref, None, None), (offs[6], sel_ref, selk_ref, selvt_ref),
                                          (offs[8], win_ref, wink_ref, winvt_ref)):
        for h in range(N_KV):
            lo = k_off + h * HEAD_DIM
            kr = _rope(proj_ref[:, lo:lo + HEAD_DIM], c, s1, s2)
            v = proj_ref[:, lo + KV_DIM:lo + KV_DIM + HEAD_DIM]
            out_ref[:, h, :] = kr
            out_ref[:, N_KV + h, :] = v
            if k_ref is not None:
                k_ref[:, h * HEAD_DIM:(h + 1) * HEAD_DIM] = kr.astype(BF16)
                vt_ref[h] = v.T.astype(BF16)


def _prep_prompt(proj, tables, conv_w, layer, *, bsz, t_len, cd, ad, tt):
    m, nmain = proj.shape
    nt = t_len // tt
    gqa = ad // HEAD_DIM // N_KV
    rows = lambda w: pl.BlockSpec((tt, w), lambda b, t: (b * nt + t, 0))
    rows3 = pl.BlockSpec((tt, 2 * N_KV, HEAD_DIM), lambda b, t: (b * nt + t, 0, 0))
    tab = pl.BlockSpec((tt, HEAD_DIM), lambda b, t: (t, 0))
    cols = lambda w: pl.BlockSpec((None, N_KV, HEAD_DIM, w), lambda b, t: (b, 0, 0, t))
    kv_rows = jax.ShapeDtypeStruct((m, 2 * N_KV, HEAD_DIM), F32)
    k_rows = jax.ShapeDtypeStruct((m, KV_DIM), BF16)
    v_cols = jax.ShapeDtypeStruct((bsz, N_KV, HEAD_DIM, t_len), BF16)
    return pl.pallas_call(
        functools.partial(_prep_prompt_kernel, cd=cd, ad=ad, tt=tt),
        grid=(bsz, nt),
        in_specs=[rows(nmain), tab, tab, tab, pl.BlockSpec((None, CONV_W, cd), lambda b, t: (layer, 0, 0))],
        out_specs=[rows(cd), cols(gqa * tt), rows3, rows3, rows3, rows(KV_DIM), cols(tt), rows(KV_DIM), cols(tt),
                   pl.BlockSpec((None, CONV_W - 1, cd), lambda b, t: (b, 0, 0))],
        out_shape=[jax.ShapeDtypeStruct((m, cd), BF16), jax.ShapeDtypeStruct((bsz, N_KV, HEAD_DIM, gqa * t_len), BF16),
                   kv_rows, kv_rows, kv_rows, k_rows, v_cols, k_rows, v_cols,
                   jax.ShapeDtypeStruct((bsz, CONV_W - 1, cd), F32)],
        scratch_shapes=[pltpu.VMEM((CONV_W - 1, cd), F32)],
        compiler_params=_cparams(("parallel", "arbitrary")),
        name="prep_prompt",
    )(proj, *tables, conv_w)


def _prep_sample_kernel(proj_ref, c_ref, s1_ref, s2_ref, cw_ref, b0_ref, b1_ref,
                        yconv_ref, q_ref, cmp_ref, sel_ref, win_ref, u_ref, *, cd, ad):
    c, s1, s2 = c_ref[...], s1_ref[...], s2_ref[...]
    offs = _proj_offsets(cd, ad)
    gb = proj_ref[:, offs[0]:offs[0] + cd]
    u = proj_ref[:, offs[1]:offs[1] + cd] * proj_ref[:, offs[2]:offs[2] + cd]
    y = cw_ref[0:1, :] * b0_ref[...] + cw_ref[1:2, :] * b1_ref[...] + cw_ref[2:3, :] * u
    yconv_ref[...] = gb * y
    u_ref[...] = u
    for h in range(ad // HEAD_DIM):
        lo = offs[3] + h * HEAD_DIM
        q_ref[:, h * HEAD_DIM:(h + 1) * HEAD_DIM] = (_rope(proj_ref[:, lo:lo + HEAD_DIM], c, s1, s2) * Q_SCALE).astype(BF16)
    for k_off, out_ref in ((offs[4], cmp_ref), (offs[6], sel_ref), (offs[8], win_ref)):
        for h in range(N_KV):
            lo = k_off + h * HEAD_DIM
            out_ref[:, h * HEAD_DIM:(h + 1) * HEAD_DIM] = _rope(proj_ref[:, lo:lo + HEAD_DIM], c, s1, s2)
            out_ref[:, KV_DIM + h * HEAD_DIM:KV_DIM + (h + 1) * HEAD_DIM] = proj_ref[:, lo + KV_DIM:lo + KV_DIM + HEAD_DIM]


def _prep_sample(proj, tables, conv_w, buf0, buf1, *, cd, ad):
    db = proj.shape[0]
    kv2 = 2 * KV_DIM
    full = lambda a: pl.BlockSpec(a.shape, lambda i: (0,) * a.ndim)
    ins = [proj, *tables, conv_w, buf0, buf1]
    outs = [jax.ShapeDtypeStruct((db, cd), F32), jax.ShapeDtypeStruct((db, ad), BF16),
            jax.ShapeDtypeStruct((db, kv2), F32), jax.ShapeDtypeStruct((db, kv2), F32),
            jax.ShapeDtypeStruct((db, kv2), F32), jax.ShapeDtypeStruct((db, cd), F32)]
    return pl.pallas_call(
        functools.partial(_prep_sample_kernel, cd=cd, ad=ad),
        grid=(1,),
        in_specs=[full(a) for a in ins],
        out_specs=[full(o) for o in outs],
        out_shape=outs,
        compiler_params=_cparams(("arbitrary",)),
        name="prep_sample",
    )(*ins)


def _cmp12_kernel(pt_ref, *refs, npg):
    del pt_ref
    pages, w_ref, o_ref = refs[:npg], refs[npg], refs[npg + 1]
    cpp = pages[0].shape[0]
    lhs = ([], [])
    for l in range(STRIDE_CMP):
        per_page = [pltpu.einshape("cjd->jcd", pr[:, l, :, :]) for pr in pages]
        for j in range(2):
            lhs[j].append(jnp.concatenate([xt[j * N_KV + h] for h in range(N_KV) for xt in per_page], axis=0))
    for j in range(2):
        rows = jnp.concatenate(lhs[j], axis=1).astype(BF16)
        out = jnp.dot(rows, w_ref[j], preferred_element_type=F32)
        o_ref[j] = out.reshape(N_KV, npg * cpp, 2 * HEAD_DIM)


def _compress_chunks(pages5, page_ids, w12, layer, *, npg=8):
    bsz, n_pages = page_ids.shape
    cpp = pages5.shape[1]
    assert cpp == 8 and pages5.shape[2] == STRIDE_CMP, "one (8, 128) tile holds a row of all chunks of a page"
    npg = _pick(n_pages, npg)

    def page_spec(i):
        return pl.BlockSpec((None, cpp, STRIDE_CMP, 2 * N_KV, HEAD_DIM),
                            lambda b, g, pt: (pt[b * n_pages + g * npg + i], 0, 0, 0, 0))

    return pl.pallas_call(
        functools.partial(_cmp12_kernel, npg=npg),
        grid_spec=pltpu.PrefetchScalarGridSpec(
            num_scalar_prefetch=1,
            grid=(bsz, n_pages // npg),
            in_specs=[page_spec(i) for i in range(npg)]
            + [pl.BlockSpec((None,) + w12.shape[1:], lambda b, g, pt: (layer, 0, 0, 0))],
            out_specs=pl.BlockSpec((None, 2, N_KV, npg * cpp, 2 * HEAD_DIM), lambda b, g, pt: (b, 0, 0, g, 0)),
        ),
        out_shape=jax.ShapeDtypeStruct((bsz, 2, N_KV, n_pages * cpp, 2 * HEAD_DIM), F32),
        compiler_params=_cparams(("parallel", "parallel")),
        name="compress_chunks",
    )(page_ids.reshape(-1), *([pages5] * npg), w12)


def _cmp_finish_kernel(p_ref, pe_ref, w1_ref, w2_ref, w2t_ref, k_ref, vt_ref, *, nc):
    for j in range(2):
        bias = jnp.dot(pe_ref[j], w1_ref[j], preferred_element_type=F32)[0:1]
        for h in range(N_KV):
            p = p_ref[j, h]
            pre = p[:, :HEAD_DIM] + pltpu.roll(p[:, HEAD_DIM:], nc - 1, 0) + bias
            act = jax.nn.gelu(pre).astype(BF16)
            if j == 0:
                k_ref[h] = jnp.dot(act, w2_ref[j], preferred_element_type=F32).astype(BF16)
            else:
                vt_ref[h] = _dot_nt(w2t_ref[j], act).astype(BF16)


def _compress_finish(p12, pe_flat, w1_flat, w2, layer):
    bsz, _, _, nc, _ = p12.shape
    per_layer = lambda a: pl.BlockSpec((None,) + a.shape[1:], lambda b: (layer,) + (0,) * (a.ndim - 1))
    return pl.pallas_call(
        functools.partial(_cmp_finish_kernel, nc=nc),
        grid=(bsz,),
        in_specs=[pl.BlockSpec((None, 2, N_KV, nc, 2 * HEAD_DIM), lambda b: (b, 0, 0, 0, 0)),
                  per_layer(pe_flat), per_layer(w1_flat), per_layer(w2), per_layer(w2)],
        out_specs=[pl.BlockSpec((None, N_KV, nc, HEAD_DIM), lambda b: (b, 0, 0, 0)),
                   pl.BlockSpec((None, N_KV, HEAD_DIM, nc), lambda b: (b, 0, 0, 0))],
        out_shape=[jax.ShapeDtypeStruct((bsz, N_KV, nc, HEAD_DIM), BF16),
                   jax.ShapeDtypeStruct((bsz, N_KV, HEAD_DIM, nc), BF16)],
        compiler_params=_cparams(("parallel",)),
        name="compress_finish",
    )(p12, pe_flat, w1_flat, w2, jnp.swapaxes(w2, 2, 3))


def _dot_nt(a, b):
    return lax.dot_general(a, b, (((1,), (1,)), ((), ())), preferred_element_type=F32)


def _split3(p):
    hi = p.astype(BF16)
    r1 = p - hi.astype(F32)
    mid = r1.astype(BF16)
    return hi, mid, (r1 - mid.astype(F32)).astype(BF16)


def _dot_exact01(p, onehot_bf16):
    return sum(jnp.dot(t, onehot_bf16, preferred_element_type=F32) for t in _split3(p))


def _overlap(n_cmp_pad, n_sel_pad, n_sel, *, cmp_axis):
    shape = (n_cmp_pad, n_sel_pad) if cmp_axis == 0 else (n_sel_pad, n_cmp_pad)
    n = lax.broadcasted_iota(jnp.int32, shape, cmp_axis) * STRIDE_CMP
    j = lax.broadcasted_iota(jnp.int32, shape, 1 - cmp_axis)
    ov = (n < j * BLK_SEL + BLK_SEL) & (n + BLK_CMP > j * BLK_SEL) & (j < n_sel)
    return ov.astype(BF16)


def _select_scores(imp, qpos, n_sel, *, axis):
    j = lax.broadcasted_iota(jnp.int32, imp.shape, axis)
    jq = qpos // BLK_SEL
    valid = j * BLK_SEL <= qpos
    forced = (j == 0) | (j == jq) | (j == jq - 1)
    score = jnp.where(valid, jnp.where(forced, FORCE, imp), -FORCE)
    return jnp.where(j < n_sel, score, -jnp.inf)


def _rank_counts(score, n_sel, *, axis):
    j = lax.broadcasted_iota(jnp.int32, score.shape, axis)
    cnt = jnp.zeros(score.shape, jnp.int32)
    for i in range(n_sel):
        si = score[i:i + 1, :] if axis == 0 else score[:, i:i + 1]
        beats = (si > score) | ((si == score) & (i < j))
        cnt = cnt + beats.astype(jnp.int32)
    return cnt


def _attn_prompt_kernel(qt_ref, kc_ref, vct_ref, ks_ref, vst_ref, kw_ref, vwt_ref, gate_ref, o_ref,
                        m_ref, l_ref, acc_ref, *, tq, tk, t_len, gqa, n_cmp, n_sel):
    i = pl.program_id(2)
    q0 = i * tq
    qpos = q0 + lax.broadcasted_iota(jnp.int32, (1, tq), 1)
    nc = kc_ref.shape[0]
    nsr = 8 * pl.cdiv(n_sel, 8)
    qt = [qt_ref[:, g * tq:(g + 1) * tq] for g in range(gqa)]

    n = lax.broadcasted_iota(jnp.int32, (nc, 1), 0)
    bias_c = jnp.where((n * STRIDE_CMP + BLK_CMP - 1 <= qpos) & (n < n_cmp), 0.0, NEG)
    has_block = (qpos >= BLK_CMP - 1).astype(F32)
    o_cmp, p_sum = [], None
    for g in range(gqa):
        s = jnp.dot(kc_ref[...], qt[g], preferred_element_type=F32) + bias_c
        e = jnp.exp2(s - jnp.max(s, axis=0, keepdims=True))
        p = e * (has_block / jnp.sum(e, axis=0, keepdims=True))
        o_cmp.append(jnp.dot(vct_ref[...], p.astype(BF16), preferred_element_type=F32))
        p_sum = p if p_sum is None else p_sum + p

    def all_started():
        j = lax.broadcasted_iota(jnp.int32, (nsr, tq), 0)
        return (j * BLK_SEL <= qpos).astype(BF16)

    def top_k():
        ov = _overlap(nc, nsr, n_sel, cmp_axis=1)
        imp = sum(jnp.dot(ov, t, preferred_element_type=F32) for t in _split3(p_sum))
        score = _select_scores(imp, qpos, n_sel, axis=0)
        return (_rank_counts(score, n_sel, axis=0) < N_SEL).astype(BF16)

    chosen = lax.cond((q0 + tq - 1) // BLK_SEL + 1 <= N_SEL, all_started, top_k)

    m_ref[...] = jnp.full(m_ref.shape, NEG, F32)
    l_ref[...] = jnp.zeros(l_ref.shape, F32)
    acc_ref[...] = jnp.zeros(acc_ref.shape, F32)

    def sel_tile(kj, carry):
        k0 = pl.multiple_of(kj * tk, tk)
        kpos = k0 + lax.broadcasted_iota(jnp.int32, (tk, 1), 0)
        blk_of_key = (k0 + lax.broadcasted_iota(jnp.int32, (tk, nsr), 0)) // BLK_SEL
        expand = (blk_of_key == lax.broadcasted_iota(jnp.int32, (tk, nsr), 1)).astype(BF16)
        in_sel = jnp.dot(expand, chosen, preferred_element_type=F32) > 0.5
        bias = jnp.where(in_sel & (kpos <= qpos), 0.0, NEG)
        keys = ks_ref[pl.ds(k0, tk), :]
        vals_t = vst_ref[:, pl.ds(k0, tk)]
        for g in range(gqa):
            s = jnp.dot(keys, qt[g], preferred_element_type=F32) + bias
            m_old = m_ref[g]
            m_new = jnp.maximum(m_old, jnp.max(s, axis=0, keepdims=True))
            a = jnp.exp2(m_old - m_new)
            p = jnp.exp2(s - m_new)
            l_ref[g] = a * l_ref[g] + jnp.sum(p, axis=0, keepdims=True)
            acc_ref[g] = a * acc_ref[g] + jnp.dot(vals_t, p.astype(BF16), preferred_element_type=F32)
            m_ref[g] = m_new
        return carry

    lax.fori_loop(0, (q0 + tq + tk - 1) // tk, sel_tile, 0)

    band = min(WINDOW + tq, t_len)
    kstart = pl.multiple_of(jnp.clip(q0 - WINDOW, 0, t_len - band), 128)
    dist = qpos - (kstart + lax.broadcasted_iota(jnp.int32, (band, 1), 0))
    bias_w = jnp.where((dist >= 0) & (dist < WINDOW), 0.0, NEG)
    keys = kw_ref[pl.ds(kstart, band), :]
    vals_t = vwt_ref[:, pl.ds(kstart, band)]

    gates_t = jax.nn.sigmoid(gate_ref[...]).T
    for g in range(gqa):
        s = jnp.dot(keys, qt[g], preferred_element_type=F32) + bias_w
        e = jnp.exp2(s - jnp.max(s, axis=0, keepdims=True))
        o_win = jnp.dot(vals_t, e.astype(BF16), preferred_element_type=F32) / jnp.sum(e, axis=0, keepdims=True)
        o_sel = acc_ref[g] / l_ref[g]
        out_t = (gates_t[3 * g:3 * g + 1] * o_cmp[g] + gates_t[3 * g + 1:3 * g + 2] * o_sel
                 + gates_t[3 * g + 2:3 * g + 3] * o_win)
        o_ref[:, g * HEAD_DIM:(g + 1) * HEAD_DIM] = out_t.T.astype(o_ref.dtype)


def _attn_prompt(qt, k_cmp, vt_cmp, sel_k, sel_vt, win_k, win_vt, gates, *, bsz, t_len, gqa, tq, tk=512):
    tk = _pick(t_len, tk)
    nt = t_len // tq
    nc = k_cmp.shape[2]
    n_cmp = (t_len - BLK_CMP) // STRIDE_CMP + 1
    n_sel = -(-t_len // BLK_SEL)
    per_head = lambda r, c: pl.BlockSpec((None, None, r, c), lambda b, h, i: (b, h, 0, 0))
    kspec = pl.BlockSpec((t_len, HEAD_DIM), lambda b, h, i: (b, h))
    return pl.pallas_call(
        functools.partial(_attn_prompt_kernel, tq=tq, tk=tk, t_len=t_len, gqa=gqa, n_cmp=n_cmp, n_sel=n_sel),
        grid=(bsz, N_KV, nt),
        in_specs=[pl.BlockSpec((None, None, HEAD_DIM, gqa * tq), lambda b, h, i: (b, h, 0, i)),
                  per_head(nc, HEAD_DIM), per_head(HEAD_DIM, nc),
                  kspec, per_head(HEAD_DIM, t_len), kspec, per_head(HEAD_DIM, t_len),
                  pl.BlockSpec((tq, HEAD_DIM), lambda b, h, i: (b * nt + i, h))],
        out_specs=pl.BlockSpec((tq, gqa * HEAD_DIM), lambda b, h, i: (b * nt + i, h)),
        out_shape=jax.ShapeDtypeStruct((bsz * t_len, N_KV * gqa * HEAD_DIM), BF16),
        scratch_shapes=[pltpu.VMEM((gqa, 1, tq), F32), pltpu.VMEM((gqa, 1, tq), F32),
                        pltpu.VMEM((gqa, HEAD_DIM, tq), F32)],
        compiler_params=_cparams(("parallel", "parallel", "arbitrary")),
        name="attn_prompt",
    )(qt, k_cmp, vt_cmp, sel_k, sel_vt, win_k, win_vt, gates)


def _attn_sample_a_kernel(q_ref, kc_ref, vct_ref, win_ref, nw_ref, ocmp_ref, owin_ref, idx_ref,
                          *, gqa, n_cmp, n_sel, qpos, wb):
    h = pl.program_id(1)
    qs = q_ref[...]
    nc = kc_ref.shape[0]

    n = lax.broadcasted_iota(jnp.int32, (1, nc), 1)
    ok = (n * STRIDE_CMP + BLK_CMP - 1 <= qpos) & (n < n_cmp)
    s = jnp.where(ok, _dot_nt(qs, kc_ref[...]), NEG)
    e = jnp.where(ok, jnp.exp2(s - jnp.max(s, axis=-1, keepdims=True)), 0.0)
    p_cmp = e / jnp.maximum(jnp.sum(e, axis=-1, keepdims=True), 1e-30)
    ocmp_ref[...] = _dot_nt(p_cmp.astype(BF16), vct_ref[...])

    psum = jnp.sum(p_cmp, axis=0, keepdims=True)
    nsl = 128 * pl.cdiv(n_sel, 128)
    imp = _dot_exact01(jnp.broadcast_to(psum, (8, nc)), _overlap(nc, nsl, n_sel, cmp_axis=0))[0:1]
    score = _select_scores(imp, jnp.full((1, 1), qpos, jnp.int32), n_sel, axis=1)
    cnt = _rank_counts(score, n_sel, axis=1)
    lane = lax.broadcasted_iota(jnp.int32, (1, nsl), 1)
    slot = lax.broadcasted_iota(jnp.int32, (N_SEL, HEAD_DIM), 0)
    idx = jnp.zeros((N_SEL, HEAD_DIM), jnp.int32)
    for k in range(N_SEL):
        blk = jnp.sum(jnp.where(cnt == k, lane, 0), axis=-1, keepdims=True)
        idx = jnp.where(slot == k, blk, idx)
    idx_ref[...] = idx

    width = 2 * N_KV
    rows = win_ref[...].reshape(wb * width, HEAD_DIM).astype(BF16)
    kn = nw_ref[pl.ds(h, 1), :]
    vn = nw_ref[pl.ds(N_KV + h, 1), :]
    col = lax.broadcasted_iota(jnp.int32, (1, wb * width), 1)
    kp = qpos - wb + col // width
    okb = (col % width == h) & (qpos - kp < WINDOW) & (kp >= 0)
    sb = jnp.where(okb, _dot_nt(qs, rows), NEG)
    sn = jnp.sum(qs.astype(F32) * kn, axis=-1, keepdims=True)
    mx = jnp.maximum(jnp.max(sb, axis=-1, keepdims=True), sn)
    pb = jnp.where(okb, jnp.exp2(sb - mx), 0.0)
    pn = jnp.exp2(sn - mx)
    den = jnp.sum(pb, axis=-1, keepdims=True) + pn
    num = jnp.dot(pltpu.roll(pb, N_KV, 1).astype(BF16), rows, preferred_element_type=F32) + pn * vn
    owin_ref[...] = num / den


def _attn_sample_a(q4, k_cmp, vt_cmp, win_all, new_win8, layer, *, gqa, qpos):
    db = q4.shape[0]
    nc = k_cmp.shape[2]
    wb = win_all.shape[1]
    n_cmp = (qpos + 1 - BLK_CMP) // STRIDE_CMP + 1
    n_sel = -(-(qpos + 1) // BLK_SEL)
    head = pl.BlockSpec((None, None, gqa, HEAD_DIM), lambda b, h: (b, h, 0, 0))
    return pl.pallas_call(
        functools.partial(_attn_sample_a_kernel, gqa=gqa, n_cmp=n_cmp, n_sel=n_sel, qpos=qpos, wb=wb),
        grid=(db, N_KV),
        in_specs=[head, pl.BlockSpec((None, None, nc, HEAD_DIM), lambda b, h: (b, h, 0, 0)),
                  pl.BlockSpec((None, None, HEAD_DIM, nc), lambda b, h: (b, h, 0, 0)),
                  pl.BlockSpec((None, wb, 2 * N_KV, HEAD_DIM), lambda b, h: (layer * db + b, 0, 0, 0)),
                  pl.BlockSpec((None, 2 * N_KV, HEAD_DIM), lambda b, h: (b, 0, 0))],
        out_specs=[head, head, pl.BlockSpec((None, None, N_SEL, HEAD_DIM), lambda b, h: (b, h, 0, 0))],
        out_shape=[jax.ShapeDtypeStruct((db, N_KV, gqa, HEAD_DIM), F32),
                   jax.ShapeDtypeStruct((db, N_KV, gqa, HEAD_DIM), F32),
                   jax.ShapeDtypeStruct((db, N_KV, N_SEL, HEAD_DIM), jnp.int32)],
        compiler_params=_cparams(("parallel", "parallel")),
        name="attn_sample_cmp_win",
    )(q4, k_cmp, vt_cmp, win_all, new_win8)


def _attn_sample_b_kernel(idx_ref, phys_ref, *refs, gqa, qpos, n_past_blocks):
    del phys_ref
    q_ref = refs[0]
    blk_refs = refs[1:1 + N_SEL]
    ns_ref, ocmp_ref, owin_ref, gate_ref, o_ref = refs[1 + N_SEL:]
    b, h = pl.program_id(0), pl.program_id(1)
    qs = q_ref[...]
    width = 2 * N_KV
    ncol = BLK_SEL * width
    col = lax.broadcasted_iota(jnp.int32, (1, ncol), 1)

    kn = ns_ref[pl.ds(h, 1), :]
    vn = ns_ref[pl.ds(N_KV + h, 1), :]
    sn = jnp.sum(qs.astype(F32) * kn, axis=-1, keepdims=True)

    scores, mx = [], sn
    for k in range(N_SEL):
        blk = idx_ref[(b * N_KV + h) * N_SEL + k]
        ok = (col % width == h) & (blk * BLK_SEL + col // width <= qpos) & (blk < n_past_blocks)
        s = jnp.where(ok, _dot_nt(qs, blk_refs[k][...].reshape(ncol, HEAD_DIM).astype(BF16)), NEG)
        scores.append(s)
        mx = jnp.maximum(mx, jnp.max(s, axis=-1, keepdims=True))
    pn = jnp.exp2(sn - mx)
    den, num = pn, pn * vn
    for k in range(N_SEL):
        p = jnp.exp2(scores[k] - mx)
        den = den + jnp.sum(p, axis=-1, keepdims=True)
        num = num + jnp.dot(pltpu.roll(p, N_KV, 1).astype(BF16),
                            blk_refs[k][...].reshape(ncol, HEAD_DIM).astype(BF16), preferred_element_type=F32)
    o_sel = num / den

    gates = jax.nn.sigmoid(gate_ref[...])
    o_cmp, o_win = ocmp_ref[...], owin_ref[...]
    rows = []
    for g in range(gqa):
        rows.append(gates[:, 3 * g:3 * g + 1] * o_cmp[g:g + 1] + gates[:, 3 * g + 1:3 * g + 2] * o_sel[g:g + 1]
                    + gates[:, 3 * g + 2:3 * g + 3] * o_win[g:g + 1])
    o_ref[...] = jnp.concatenate(rows, axis=0)


def _attn_sample_b(idx, phys, q4, sel_blocks, new_sel8, o_cmp, o_win, gates4, *, gqa, qpos, n_past_blocks):
    db = q4.shape[0]
    head = pl.BlockSpec((None, None, gqa, HEAD_DIM), lambda b, h, i, p: (b, h, 0, 0))

    def blk_spec(k):
        return pl.BlockSpec((None, BLK_SEL, 2 * N_KV, HEAD_DIM),
                            lambda b, h, i, p: (p[(b * N_KV + h) * N_SEL + k], 0, 0, 0))

    return pl.pallas_call(
        functools.partial(_attn_sample_b_kernel, gqa=gqa, qpos=qpos, n_past_blocks=n_past_blocks),
        grid_spec=pltpu.PrefetchScalarGridSpec(
            num_scalar_prefetch=2,
            grid=(db, N_KV),
            in_specs=[head] + [blk_spec(k) for k in range(N_SEL)]
            + [pl.BlockSpec((None, 2 * N_KV, HEAD_DIM), lambda b, h, i, p: (b, 0, 0)),
               head, head,
               pl.BlockSpec((None, None, 1, HEAD_DIM), lambda b, h, i, p: (b, h, 0, 0))],
            out_specs=head,
        ),
        out_shape=jax.ShapeDtypeStruct((db, N_KV, gqa, HEAD_DIM), F32),
        compiler_params=_cparams(("parallel", "parallel")),
        name="attn_sample_sel",
    )(idx.reshape(-1), phys.reshape(-1), q4, *([sel_blocks] * N_SEL), new_sel8, o_cmp, o_win, gates4)


def _roll_window_kernel(s_ref, n_ref, o_ref, *, wb):
    o_ref[0:wb - 1] = s_ref[1:wb]
    o_ref[wb - 1] = n_ref[...]


def _roll_window(state, new_rows):
    n, wb = state.shape[:2]
    tile = (2 * N_KV, HEAD_DIM)
    return pl.pallas_call(
        functools.partial(_roll_window_kernel, wb=wb),
        grid=(n,),
        in_specs=[pl.BlockSpec((None, wb) + tile, lambda b: (b, 0, 0, 0)), pl.BlockSpec((None,) + tile, lambda b: (b, 0, 0))],
        out_specs=pl.BlockSpec((None, wb) + tile, lambda b: (b, 0, 0, 0)),
        out_shape=jax.ShapeDtypeStruct(state.shape, state.dtype),
        compiler_params=_cparams(("parallel",)),
        name="roll_window",
    )(state, new_rows)


def _pad_rows(a, rows):
    return jnp.pad(a, ((0, rows - a.shape[0]), (0, 0)))


def kernel(x_prompt, x_sample, cache_cmp, cache_sel, state_win, state_conv, page_table, w_in, conv_w, cmp_pe, cmp_w1, cmp_w2, w_out, ln1_g, ln1_b, w_up, w_down, ln2_g, ln2_b):
    bsz, t_len, d = x_prompt.shape
    db, dec_seq, _ = x_sample.shape
    depth = w_in.shape[0]
    cd = state_conv.shape[-1]
    ad = d - cd
    gqa = ad // HEAD_DIM // N_KV
    pool, page_rows = cache_cmp.shape[1], cache_cmp.shape[2]
    n_pages = page_table.shape[1]
    past = n_pages * page_rows
    wb = state_win.shape[2]
    assert dec_seq == 1 and wb == WINDOW and t_len >= WINDOW and 3 * gqa <= HEAD_DIM and db <= SAMPLE_ROWS
    assert page_rows % BLK_SEL == 0 and t_len % page_rows == 0
    alpha = (2.0 * depth) ** 0.25
    nmain = 3 * cd + ad + 6 * KV_DIM
    cpp = page_rows // STRIDE_CMP
    bpp = page_rows // BLK_SEL
    tile = (2 * N_KV, HEAD_DIM)
    tq = _pick(t_len, ATTN_TQ)

    tab_p = _rope_tables(jnp.arange(t_len, dtype=jnp.int32))
    tab_s = _rope_tables(jnp.full((db,), past, jnp.int32))
    pt_prompt = jnp.arange(bsz * (t_len // page_rows), dtype=jnp.int32).reshape(bsz, t_len // page_rows)

    wg = w_in[:, :, nmain:].reshape(depth, d, N_KV, 3 * gqa)
    wg = jnp.pad(wg, ((0, 0), (0, 0), (0, 0), (0, HEAD_DIM - 3 * gqa))).reshape(depth, d, KV_DIM).astype(BF16)
    half = STRIDE_CMP * HEAD_DIM
    w1_flat = cmp_w1.reshape(depth, 2, BLK_CMP * HEAD_DIM, HEAD_DIM).astype(BF16)
    w12 = jnp.concatenate([w1_flat[:, :, :half], w1_flat[:, :, half:]], axis=3)
    pe_flat = jnp.broadcast_to(cmp_pe.reshape(depth, 2, 1, BLK_CMP * HEAD_DIM), (depth, 2, 8, BLK_CMP * HEAD_DIM)).astype(BF16)
    w2 = cmp_w2.astype(BF16)
    w_down_bf = w_down.astype(BF16)
    w_in_nk = jnp.swapaxes(w_in, 1, 2)
    ln = [a.reshape(depth, 1, d) for a in (ln1_g, ln1_b, ln2_g, ln2_b)]

    cmp_pages = cache_cmp.reshape(depth * pool, cpp, STRIDE_CMP, *tile)
    sel_blocks = cache_sel.reshape(depth * pool * bpp, BLK_SEL, *tile)
    win_all = state_win.reshape(depth * db, wb, *tile)

    xp = x_prompt.reshape(bsz * t_len, d)
    xp_bf = xp.astype(BF16)
    xs = _pad_rows(x_sample.reshape(db, d), SAMPLE_ROWS)
    xs_bf = xs.astype(BF16)

    outs = {k: [] for k in ("cmp_p", "sel_p", "win_p", "conv_p", "cmp_s", "sel_s", "win_s", "conv_s")}
    for l in range(depth):
        proj, proj_s = _matmul_ws([xp_bf], [xs_bf], w_in_nk, l, n_out=nmain, out_dtype=F32, w_is_nk=True, name="mm_in")
        gates = _matmul(xp_bf, wg, l, out_dtype=F32, bn=KV_DIM, name="mm_gate")
        gates_s = _matmul(xs_bf, wg, l, out_dtype=F32, bn=KV_DIM, name="mm_gate_s")[:db]

        yconv, qt, new_cmp, new_sel, new_win, sel_k, sel_vt, win_k, win_vt, new_conv = _prep_prompt(
            proj, tab_p, conv_w, l, bsz=bsz, t_len=t_len, cd=cd, ad=ad, tt=tq)
        p12 = _compress_chunks(new_cmp.reshape(-1, cpp, STRIDE_CMP, *tile), pt_prompt, w12, l)
        k_cmp, vt_cmp = _compress_finish(p12, pe_flat, w1_flat, w2, l)
        o_attn = _attn_prompt(qt, k_cmp, vt_cmp, sel_k, sel_vt, win_k, win_vt, gates,
                              bsz=bsz, t_len=t_len, gqa=gqa, tq=tq)

        yconv_s, q_s, cmp_row, sel_row, win_row, u_s = _prep_sample(
            proj_s[:db], tab_s, conv_w[l], state_conv[l, :, 0], state_conv[l, :, 1], cd=cd, ad=ad)
        p12_s = _compress_chunks(cmp_pages, page_table + l * pool, w12, l)
        k_cmp_s, vt_cmp_s = _compress_finish(p12_s, pe_flat, w1_flat, w2, l)
        q4 = q_s.reshape(db, N_KV, gqa, HEAD_DIM)
        o_cmp_s, o_win_s, idx = _attn_sample_a(q4, k_cmp_s, vt_cmp_s, win_all, win_row.reshape(db, *tile), l,
                                               gqa=gqa, qpos=past)
        idx = idx[..., 0]
        page = jnp.minimum(idx // bpp, n_pages - 1).reshape(db, N_KV * N_SEL)
        phys = (jnp.take_along_axis(page_table, page, axis=1) + l * pool) * bpp + idx.reshape(db, -1) % bpp
        o_attn_s = _attn_sample_b(idx, phys, q4, sel_blocks, sel_row.reshape(db, *tile), o_cmp_s, o_win_s,
                                  gates_s.reshape(db, N_KV, 1, HEAD_DIM), gqa=gqa, qpos=past, n_past_blocks=n_pages * bpp)
        yconv_s = _pad_rows(yconv_s, SAMPLE_ROWS).astype(BF16)
        o_attn_s = _pad_rows(o_attn_s.reshape(db, ad), SAMPLE_ROWS).astype(BF16)

        h1, h1_s = _matmul_ws([yconv, o_attn], [yconv_s, o_attn_s], w_out, l, n_out=d, out_dtype=F32,
                              res=xp, res_s=xs, alpha=alpha, name="mm_out")
        x1, x1_bf = _layer_norm(h1, ln[0], ln[1], l)
        x1_s, x1_s_bf = _layer_norm(h1_s, ln[0], ln[1], l)
        up, up_s = _matmul_ws([x1_bf], [x1_s_bf], w_up, l, n_out=w_up.shape[2], out_dtype=BF16, act="relu2", name="mm_up")
        h2 = _matmul(up, w_down_bf, l, out_dtype=F32, res=x1, alpha=alpha, bk=2048, name="mm_down")
        h2_s = _matmul(up_s, w_down_bf, l, out_dtype=F32, res=x1_s, alpha=alpha, bk=2048, name="mm_down_s")
        xp, xp_bf = _layer_norm(h2, ln[2], ln[3], l)
        xs, xs_bf = _layer_norm(h2_s, ln[2], ln[3], l)

        kv_shape = (2, N_KV, HEAD_DIM)
        outs["cmp_p"].append(new_cmp.reshape(bsz, t_len, *kv_shape))
        outs["sel_p"].append(new_sel.reshape(bsz, t_len, *kv_shape))
        outs["win_p"].append(new_win.reshape(bsz, t_len, *kv_shape)[:, t_len - min(WINDOW, t_len):])
        outs["conv_p"].append(new_conv)
        outs["cmp_s"].append(cmp_row.reshape(db, 1, *kv_shape))
        outs["sel_s"].append(sel_row.reshape(db, 1, *kv_shape))
        outs["win_s"].append(win_row.reshape(db, *tile))
        outs["conv_s"].append(jnp.stack([state_conv[l, :, 1], u_s], axis=1))

    st = {k: jnp.stack(v) for k, v in outs.items()}
    new_win_s = _roll_window(win_all, st["win_s"].reshape(depth * db, *tile)).reshape(depth, db, wb, 2, N_KV, HEAD_DIM)
    return (xp.reshape(bsz, t_len, d), xs[:db].reshape(db, 1, d), st["cmp_p"], st["sel_p"], st["win_p"], st["conv_p"],
            st["cmp_s"], st["sel_s"], new_win_s, st["conv_s"])
```

```python
import functools
import math

import jax
import jax.numpy as jnp
from jax import lax
from jax.experimental import pallas as pl
from jax.experimental.pallas import tpu as pltpu

F32 = jnp.float32
BF16 = jnp.bfloat16

HEAD_DIM = 128
N_KV = 4
KV_DIM = N_KV * HEAD_DIM
ROT_DIM = HEAD_DIM // 4
ROPE_THETA = 500000.0
BLK_CMP = 32
STRIDE_CMP = 16
BLK_SEL = 64
N_SEL = 16
WINDOW = 512
CONV_W = 3
LN_EPS = 1e-5
FORCE = 1e4
NEG = -1e30
Q_SCALE = HEAD_DIM ** -0.5 * math.log2(math.e)
SAMPLE_ROWS = 16
ATTN_TQ = 256

VMEM_LIMIT_BYTES = 56 * 1024 * 1024


def _cparams(sem):
    return pltpu.CompilerParams(dimension_semantics=sem, vmem_limit_bytes=VMEM_LIMIT_BYTES)


def _pick(n, pref):
    if n <= pref:
        return n
    t = pref
    while n % t:
        t //= 2
    return t


def _epilogue(acc, res, *, act, alpha):
    if act == "relu2":
        acc = jnp.square(jnp.maximum(acc, 0.0))
    if res is not None:
        acc = alpha * res + acc
    return acc


def _mm_ws_kernel(*refs, n_x, ksplit, act, alpha, has_res, w_is_nk):
    w_ref = refs[0]
    x_refs = refs[1:1 + n_x]
    xs_refs = refs[1 + n_x:1 + 2 * n_x]
    pos = 1 + 2 * n_x
    res_ref, ress_ref = (refs[pos], refs[pos + 1]) if has_res else (None, None)
    pos += 2 * has_res
    o_ref, os_ref, wbf_ref = refs[pos], refs[pos + 1], refs[pos + 2]

    def product(x_list):
        acc = None
        for x_ref, (k0, k1) in zip(x_list, ksplit):
            part = jnp.dot(x_ref[...], wbf_ref[k0:k1, :], preferred_element_type=F32)
            acc = part if acc is None else acc + part
        return acc

    @pl.when(pl.program_id(1) == 0)
    def _():
        w = w_ref[...]
        wbf_ref[...] = (w.T if w_is_nk else w).astype(BF16)
        res_s = ress_ref[...] if has_res else None
        os_ref[...] = _epilogue(product(xs_refs), res_s, act=act, alpha=alpha).astype(os_ref.dtype)

    res = res_ref[...] if has_res else None
    o_ref[...] = _epilogue(product(x_refs), res, act=act, alpha=alpha).astype(o_ref.dtype)


def _matmul_ws(xs, xs_s, w, layer, *, n_out, out_dtype, act=None, res=None, res_s=None, alpha=1.0,
               w_is_nk=False, bm=1024, bn=512, name="mm"):
    m = xs[0].shape[0]
    ms = xs_s[0].shape[0]
    kdim = w.shape[2] if w_is_nk else w.shape[1]
    bm, bn = _pick(m, bm), _pick(n_out, bn)
    ksplit, k0 = [], 0
    for x in xs:
        ksplit.append((k0, k0 + x.shape[1]))
        k0 += x.shape[1]
    assert k0 == kdim
    if w_is_nk:
        in_specs = [pl.BlockSpec((None, bn, kdim), lambda j, i: (layer, j, 0))]
    else:
        in_specs = [pl.BlockSpec((None, kdim, bn), lambda j, i: (layer, 0, j))]
    in_specs += [pl.BlockSpec((bm, x.shape[1]), lambda j, i: (i, 0)) for x in xs]
    in_specs += [pl.BlockSpec((ms, x.shape[1]), lambda j, i: (0, 0)) for x in xs_s]
    args = [w, *xs, *xs_s]
    if res is not None:
        in_specs += [pl.BlockSpec((bm, bn), lambda j, i: (i, j)), pl.BlockSpec((ms, bn), lambda j, i: (0, j))]
        args += [res, res_s]
    return pl.pallas_call(
        functools.partial(_mm_ws_kernel, n_x=len(xs), ksplit=tuple(ksplit), act=act, alpha=alpha,
                          has_res=res is not None, w_is_nk=w_is_nk),
        grid=(n_out // bn, m // bm),
        in_specs=in_specs,
        out_specs=[pl.BlockSpec((bm, bn), lambda j, i: (i, j)), pl.BlockSpec((ms, bn), lambda j, i: (0, j))],
        out_shape=[jax.ShapeDtypeStruct((m, n_out), out_dtype), jax.ShapeDtypeStruct((ms, n_out), out_dtype)],
        scratch_shapes=[pltpu.VMEM((kdim, bn), BF16)],
        compiler_params=_cparams(("arbitrary", "arbitrary")),
        name=name,
    )(*args)


def _mm_kernel(*refs, nk, act, alpha, has_res):
    x_ref, w_ref = refs[0], refs[1]
    res_ref = refs[2] if has_res else None
    o_ref = refs[2 + has_res]
    acc_ref = refs[3 + has_res] if nk > 1 else None
    part = jnp.dot(x_ref[...], w_ref[...], preferred_element_type=F32)

    def finish(acc):
        res = res_ref[...] if has_res else None
        o_ref[...] = _epilogue(acc, res, act=act, alpha=alpha).astype(o_ref.dtype)

    if nk == 1:
        finish(part)
    else:
        k = pl.program_id(2)

        @pl.when(k == 0)
        def _():
            acc_ref[...] = part

        @pl.when(k > 0)
        def _():
            acc_ref[...] += part

        @pl.when(k == nk - 1)
        def _():
            finish(acc_ref[...])


def _matmul(x, w, layer=None, *, out_dtype, act=None, res=None, alpha=1.0, bm=1024, bn=1024, bk=4096, name="mm"):
    m, kdim = x.shape
    n = w.shape[-1]
    bm, bn, bk = _pick(m, bm), _pick(n, bn), _pick(kdim, bk)
    nk = kdim // bk
    if layer is None:
        w_spec = pl.BlockSpec((bk, bn), lambda i, j, k: (k, j))
    else:
        w_spec = pl.BlockSpec((None, bk, bn), lambda i, j, k: (layer, k, j))
    in_specs = [pl.BlockSpec((bm, bk), lambda i, j, k: (i, k)), w_spec]
    args = [x, w]
    if res is not None:
        in_specs.append(pl.BlockSpec((bm, bn), lambda i, j, k: (i, j)))
        args.append(res)
    return pl.pallas_call(
        functools.partial(_mm_kernel, nk=nk, act=act, alpha=alpha, has_res=res is not None),
        grid=(m // bm, n // bn, nk),
        in_specs=in_specs,
        out_specs=pl.BlockSpec((bm, bn), lambda i, j, k: (i, j)),
        out_shape=jax.ShapeDtypeStruct((m, n), out_dtype),
        scratch_shapes=[pltpu.VMEM((bm, bn), F32)] if nk > 1 else [],
        compiler_params=_cparams(("parallel", "parallel", "arbitrary")),
        name=name,
    )(*args)


def _ln_kernel(h_ref, g_ref, b_ref, o_ref, obf_ref):
    h = h_ref[...]
    mu = jnp.mean(h, axis=-1, keepdims=True)
    d = h - mu
    var = jnp.mean(d * d, axis=-1, keepdims=True)
    y = d * lax.rsqrt(var + LN_EPS) * g_ref[...] + b_ref[...]
    o_ref[...] = y
    obf_ref[...] = y.astype(BF16)


def _layer_norm(h, g, b, layer, *, bm=256):
    m, d = h.shape
    bm = _pick(m, bm)
    row = pl.BlockSpec((bm, d), lambda i: (i, 0))
    vec = pl.BlockSpec((None, 1, d), lambda i: (layer, 0, 0))
    return pl.pallas_call(
        _ln_kernel,
        grid=(m // bm,),
        in_specs=[row, vec, vec],
        out_specs=[row, row],
        out_shape=[jax.ShapeDtypeStruct((m, d), F32), jax.ShapeDtypeStruct((m, d), BF16)],
        compiler_params=_cparams(("parallel",)),
        name="layer_norm",
    )(h, g, b)


def _rope_tables(pos):
    half = ROT_DIM // 2
    inv = ROPE_THETA ** (-jnp.arange(half, dtype=F32) / half)
    ang = pos.astype(F32)[:, None] * inv[None, :]
    cos, sin = jnp.cos(ang), jnp.sin(ang)
    n = pos.shape[0]
    one = jnp.ones((n, HEAD_DIM - ROT_DIM), F32)
    zero = jnp.zeros((n, HEAD_DIM - ROT_DIM), F32)
    zh = jnp.zeros((n, half), F32)
    c = jnp.concatenate([cos, cos, one], 1)
    s1 = jnp.concatenate([zh, sin, zero], 1)
    s2 = jnp.concatenate([-sin, zh, zero], 1)
    return c, s1, s2


def _rope(x, c, s1, s2):
    return x * c + pltpu.roll(x, ROT_DIM // 2, 1) * s1 + pltpu.roll(x, HEAD_DIM - ROT_DIM // 2, 1) * s2


def _proj_offsets(cd, ad):
    offs = [0, cd, 2 * cd, 3 * cd, 3 * cd + ad]
    for _ in range(5):
        offs.append(offs[-1] + KV_DIM)
    return offs


def _prep_prompt_kernel(proj_ref, c_ref, s1_ref, s2_ref, cw_ref,
                        yconv_ref, qt_ref, cmp_ref, sel_ref, win_ref, selk_ref, selvt_ref, wink_ref, winvt_ref,
                        nconv_ref, carry_ref, *, cd, ad, tt):
    t = pl.program_id(1)
    c, s1, s2 = c_ref[...], s1_ref[...], s2_ref[...]
    offs = _proj_offsets(cd, ad)

    @pl.when(t == 0)
    def _():
        carry_ref[...] = jnp.zeros_like(carry_ref)

    gb = proj_ref[:, offs[0]:offs[0] + cd]
    u = proj_ref[:, offs[1]:offs[1] + cd] * proj_ref[:, offs[2]:offs[2] + cd]
    row = lax.broadcasted_iota(jnp.int32, (tt, cd), 0)
    c0 = carry_ref[0:1, :]
    c1 = carry_ref[1:2, :]
    u1 = jnp.where(row == 0, c1, pltpu.roll(u, 1, 0))
    u2 = jnp.where(row == 0, c0, jnp.where(row == 1, c1, pltpu.roll(u, 2, 0)))
    y = cw_ref[0:1, :] * u2 + cw_ref[1:2, :] * u1 + cw_ref[2:3, :] * u
    yconv_ref[...] = (gb * y).astype(BF16)
    tail = (proj_ref[tt - 2:tt, offs[1]:offs[1] + cd] * proj_ref[tt - 2:tt, offs[2]:offs[2] + cd])
    carry_ref[...] = tail
    nconv_ref[...] = tail

    gqa = ad // HEAD_DIM // N_KV
    for h in range(ad // HEAD_DIM):
        lo = offs[3] + h * HEAD_DIM
        g = h % gqa
        qt = (_rope(proj_ref[:, lo:lo + HEAD_DIM], c, s1, s2) * Q_SCALE).T
        qt_ref[h // gqa, :, g * tt:(g + 1) * tt] = qt.astype(BF16)

    for k_off, out_ref, k_ref, vt_ref in ((offs[4], cmp_ref, None, None), (offs[6], sel_ref, selk_ref, selvt_ref),
                                          (offs[8], win_ref, wink_ref, winvt_ref)):
        for h in range(N_KV):
            lo = k_off + h * HEAD_DIM
            kr = _rope(proj_ref[:, lo:lo + HEAD_DIM], c, s1, s2)
            v = proj_ref[:, lo + KV_DIM:lo + KV_DIM + HEAD_DIM]
            out_ref[:, h, :] = kr
            out_ref[:, N_KV + h, :] = v
            if k_ref is not None:
                k_ref[:, h * HEAD_DIM:(h + 1) * HEAD_DIM] = kr.astype(BF16)
                vt_ref[h] = v.T.astype(BF16)


def _prep_prompt(proj, tables, conv_w, layer, *, bsz, t_len, cd, ad, tt):
    m, nmain = proj.shape
    nt = t_len // tt
    gqa = ad // HEAD_DIM // N_KV
    rows = lambda w: pl.BlockSpec((tt, w), lambda b, t: (b * nt + t, 0))
    rows3 = pl.BlockSpec((tt, 2 * N_KV, HEAD_DIM), lambda b, t: (b * nt + t, 0, 0))
    tab = pl.BlockSpec((tt, HEAD_DIM), lambda b, t: (t, 0))
    cols = lambda w: pl.BlockSpec((None, N_KV, HEAD_DIM, w), lambda b, t: (b, 0, 0, t))
    kv_rows = jax.ShapeDtypeStruct((m, 2 * N_KV, HEAD_DIM), F32)
    k_rows = jax.ShapeDtypeStruct((m, KV_DIM), BF16)
    v_cols = jax.ShapeDtypeStruct((bsz, N_KV, HEAD_DIM, t_len), BF16)
    return pl.pallas_call(
        functools.partial(_prep_prompt_kernel, cd=cd, ad=ad, tt=tt),
        grid=(bsz, nt),
        in_specs=[rows(nmain), tab, tab, tab, pl.BlockSpec((None, CONV_W, cd), lambda b, t: (layer, 0, 0))],
        out_specs=[rows(cd), cols(gqa * tt), rows3, rows3, rows3, rows(KV_DIM), cols(tt), rows(KV_DIM), cols(tt),
                   pl.BlockSpec((None, CONV_W - 1, cd), lambda b, t: (b, 0, 0))],
        out_shape=[jax.ShapeDtypeStruct((m, cd), BF16), jax.ShapeDtypeStruct((bsz, N_KV, HEAD_DIM, gqa * t_len), BF16),
                   kv_rows, kv_rows, kv_rows, k_rows, v_cols, k_rows, v_cols,
                   jax.ShapeDtypeStruct((bsz, CONV_W - 1, cd), F32)],
        scratch_shapes=[pltpu.VMEM((CONV_W - 1, cd), F32)],
        compiler_params=_cparams(("parallel", "arbitrary")),
        name="prep_prompt",
    )(proj, *tables, conv_w)


def _prep_sample_kernel(proj_ref, c_ref, s1_ref, s2_ref, cw_ref, b0_ref, b1_ref,
                        yconv_ref, q_ref, cmp_ref, sel_ref, win_ref, u_ref, *, cd, ad):
    c, s1, s2 = c_ref[...], s1_ref[...], s2_ref[...]
    offs = _proj_offsets(cd, ad)
    gb = proj_ref[:, offs[0]:offs[0] + cd]
    u = proj_ref[:, offs[1]:offs[1] + cd] * proj_ref[:, offs[2]:offs[2] + cd]
    y = cw_ref[0:1, :] * b0_ref[...] + cw_ref[1:2, :] * b1_ref[...] + cw_ref[2:3, :] * u
    yconv_ref[...] = gb * y
    u_ref[...] = u
    for h in range(ad // HEAD_DIM):
        lo = offs[3] + h * HEAD_DIM
        q_ref[:, h * HEAD_DIM:(h + 1) * HEAD_DIM] = (_rope(proj_ref[:, lo:lo + HEAD_DIM], c, s1, s2) * Q_SCALE).astype(BF16)
    for k_off, out_ref in ((offs[4], cmp_ref), (offs[6], sel_ref), (offs[8], win_ref)):
        for h in range(N_KV):
            lo = k_off + h * HEAD_DIM
            out_ref[:, h * HEAD_DIM:(h + 1) * HEAD_DIM] = _rope(proj_ref[:, lo:lo + HEAD_DIM], c, s1, s2)
            out_ref[:, KV_DIM + h * HEAD_DIM:KV_DIM + (h + 1) * HEAD_DIM] = proj_ref[:, lo + KV_DIM:lo + KV_DIM + HEAD_DIM]


def _prep_sample(proj, tables, conv_w, buf0, buf1, *, cd, ad):
    db = proj.shape[0]
    kv2 = 2 * KV_DIM
    full = lambda a: pl.BlockSpec(a.shape, lambda i: (0,) * a.ndim)
    ins = [proj, *tables, conv_w, buf0, buf1]
    outs = [jax.ShapeDtypeStruct((db, cd), F32), jax.ShapeDtypeStruct((db, ad), BF16),
            jax.ShapeDtypeStruct((db, kv2), F32), jax.ShapeDtypeStruct((db, kv2), F32),
            jax.ShapeDtypeStruct((db, kv2), F32), jax.ShapeDtypeStruct((db, cd), F32)]
    return pl.pallas_call(
        functools.partial(_prep_sample_kernel, cd=cd, ad=ad),
        grid=(1,),
        in_specs=[full(a) for a in ins],
        out_specs=[full(o) for o in outs],
        out_shape=outs,
        compiler_params=_cparams(("arbitrary",)),
        name="prep_sample",
    )(*ins)


def _cmp12_kernel(pt_ref, *refs, npg):
    del pt_ref
    pages, w_ref, o_ref = refs[:npg], refs[npg], refs[npg + 1]
    cpp = pages[0].shape[0]
    lhs = ([], [])
    for l in range(STRIDE_CMP):
        per_page = [pltpu.einshape("cjd->jcd", pr[:, l, :, :]) for pr in pages]
        for j in range(2):
            lhs[j].append(jnp.concatenate([xt[j * N_KV + h] for h in range(N_KV) for xt in per_page], axis=0))
    for j in range(2):
        rows = jnp.concatenate(lhs[j], axis=1).astype(BF16)
        out = jnp.dot(rows, w_ref[j], preferred_element_type=F32)
        o_ref[j] = out.reshape(N_KV, npg * cpp, 2 * HEAD_DIM)


def _compress_chunks(pages5, page_ids, w12, layer, *, npg=8):
    bsz, n_pages = page_ids.shape
    cpp = pages5.shape[1]
    assert cpp == 8 and pages5.shape[2] == STRIDE_CMP, "one (8, 128) tile holds a row of all chunks of a page"
    npg = _pick(n_pages, npg)

    def page_spec(i):
        return pl.BlockSpec((None, cpp, STRIDE_CMP, 2 * N_KV, HEAD_DIM),
                            lambda b, g, pt: (pt[b * n_pages + g * npg + i], 0, 0, 0, 0))

    return pl.pallas_call(
        functools.partial(_cmp12_kernel, npg=npg),
        grid_spec=pltpu.PrefetchScalarGridSpec(
            num_scalar_prefetch=1,
            grid=(bsz, n_pages // npg),
            in_specs=[page_spec(i) for i in range(npg)]
            + [pl.BlockSpec((None,) + w12.shape[1:], lambda b, g, pt: (layer, 0, 0, 0))],
            out_specs=pl.BlockSpec((None, 2, N_KV, npg * cpp, 2 * HEAD_DIM), lambda b, g, pt: (b, 0, 0, g, 0)),
        ),
        out_shape=jax.ShapeDtypeStruct((bsz, 2, N_KV, n_pages * cpp, 2 * HEAD_DIM), F32),
        compiler_params=_cparams(("parallel", "parallel")),
        name="compress_chunks",
    )(page_ids.reshape(-1), *([pages5] * npg), w12)


def _cmp_finish_kernel(p_ref, pe_ref, w1_ref, w2_ref, w2t_ref, k_ref, vt_ref, *, nc):
    for j in range(2):
        bias = jnp.dot(pe_ref[j], w1_ref[j], preferred_element_type=F32)[0:1]
        for h in range(N_KV):
            p = p_ref[j, h]
            pre = p[:, :HEAD_DIM] + pltpu.roll(p[:, HEAD_DIM:], nc - 1, 0) + bias
            act = jax.nn.gelu(pre).astype(BF16)
            if j == 0:
                k_ref[h] = jnp.dot(act, w2_ref[j], preferred_element_type=F32).astype(BF16)
            else:
                vt_ref[h] = _dot_nt(w2t_ref[j], act).astype(BF16)


def _compress_finish(p12, pe_flat, w1_flat, w2, layer):
    bsz, _, _, nc, _ = p12.shape
    per_layer = lambda a: pl.BlockSpec((None,) + a.shape[1:], lambda b: (layer,) + (0,) * (a.ndim - 1))
    return pl.pallas_call(
        functools.partial(_cmp_finish_kernel, nc=nc),
        grid=(bsz,),
        in_specs=[pl.BlockSpec((None, 2, N_KV, nc, 2 * HEAD_DIM), lambda b: (b, 0, 0, 0, 0)),
                  per_layer(pe_flat), per_layer(w1_flat), per_layer(w2), per_layer(w2)],
        out_specs=[pl.BlockSpec((None, N_KV, nc, HEAD_DIM), lambda b: (b, 0, 0, 0)),
                   pl.BlockSpec((None, N_KV, HEAD_DIM, nc), lambda b: (b, 0, 0, 0))],
        out_shape=[jax.ShapeDtypeStruct((bsz, N_KV, nc, HEAD_DIM), BF16),
                   jax.ShapeDtypeStruct((bsz, N_KV, HEAD_DIM, nc), BF16)],
        compiler_params=_cparams(("parallel",)),
        name="compress_finish",
    )(p12, pe_flat, w1_flat, w2, jnp.swapaxes(w2, 2, 3))


def _dot_nt(a, b):
    return lax.dot_general(a, b, (((1,), (1,)), ((), ())), preferred_element_type=F32)


def _split3(p):
    hi = p.astype(BF16)
    r1 = p - hi.astype(F32)
    mid = r1.astype(BF16)
    return hi, mid, (r1 - mid.astype(F32)).astype(BF16)


def _dot_exact01(p, onehot_bf16):
    return sum(jnp.dot(t, onehot_bf16, preferred_element_type=F32) for t in _split3(p))


def _overlap(n_cmp_pad, n_sel_pad, n_sel, *, cmp_axis):
    shape = (n_cmp_pad, n_sel_pad) if cmp_axis == 0 else (n_sel_pad, n_cmp_pad)
    n = lax.broadcasted_iota(jnp.int32, shape, cmp_axis) * STRIDE_CMP
    j = lax.broadcasted_iota(jnp.int32, shape, 1 - cmp_axis)
    ov = (n < j * BLK_SEL + BLK_SEL) & (n + BLK_CMP > j * BLK_SEL) & (j < n_sel)
    return ov.astype(BF16)


def _select_scores(imp, qpos, n_sel, *, axis):
    j = lax.broadcasted_iota(jnp.int32, imp.shape, axis)
    jq = qpos // BLK_SEL
    valid = j * BLK_SEL <= qpos
    forced = (j == 0) | (j == jq) | (j == jq - 1)
    score = jnp.where(valid, jnp.where(forced, FORCE, imp), -FORCE)
    return jnp.where(j < n_sel, score, -jnp.inf)


def _rank_counts(score, n_sel, *, axis):
    j = lax.broadcasted_iota(jnp.int32, score.shape, axis)
    cnt = jnp.zeros(score.shape, jnp.int32)
    for i in range(n_sel):
        si = score[i:i + 1, :] if axis == 0 else score[:, i:i + 1]
        beats = (si > score) | ((si == score) & (i < j))
        cnt = cnt + beats.astype(jnp.int32)
    return cnt


def _attn_prompt_kernel(qt_ref, kc_ref, vct_ref, ks_ref, vst_ref, kw_ref, vwt_ref, gate_ref, o_ref,
                        m_ref, l_ref, acc_ref, *, tq, tk, t_len, gqa, n_cmp, n_sel):
    i = pl.program_id(2)
    q0 = i * tq
    qpos = q0 + lax.broadcasted_iota(jnp.int32, (1, tq), 1)
    nc = kc_ref.shape[0]
    nsr = 8 * pl.cdiv(n_sel, 8)
    qt = qt_ref[...]

    def per_head(x):
        return jnp.concatenate([x] * gqa, axis=1)

    def head(x, g):
        return x[:, g * tq:(g + 1) * tq]

    n = lax.broadcasted_iota(jnp.int32, (nc, 1), 0)
    bias_c = jnp.where((n * STRIDE_CMP + BLK_CMP - 1 <= qpos) & (n < n_cmp), 0.0, NEG)
    has_block = (qpos >= BLK_CMP - 1).astype(F32)
    s = jnp.dot(kc_ref[...], qt, preferred_element_type=F32) + per_head(bias_c)
    e = jnp.exp2(s - jnp.max(s, axis=0, keepdims=True))
    p_cmp = e * (per_head(has_block) / jnp.sum(e, axis=0, keepdims=True))
    o_cmp = jnp.dot(vct_ref[...], p_cmp.astype(BF16), preferred_element_type=F32)

    def all_started():
        j = lax.broadcasted_iota(jnp.int32, (nsr, tq), 0)
        return jnp.where(j * BLK_SEL <= qpos, 0.0, NEG)

    def top_k():
        p_sum = sum(head(p_cmp, g) for g in range(gqa))
        ov = _overlap(nc, nsr, n_sel, cmp_axis=1)
        imp = sum(jnp.dot(ov, t, preferred_element_type=F32) for t in _split3(p_sum))
        score = _select_scores(imp, qpos, n_sel, axis=0)
        return jnp.where(_rank_counts(score, n_sel, axis=0) < N_SEL, 0.0, NEG)

    sel_bias = lax.cond((q0 + tq - 1) // BLK_SEL + 1 <= N_SEL, all_started, top_k).astype(BF16)

    sel_bias = jnp.concatenate([sel_bias, jnp.zeros((HEAD_DIM - nsr, tq), BF16)], axis=0)
    qt_sel = jnp.concatenate([qt, per_head(sel_bias)], axis=0)

    m_ref[...] = jnp.full(m_ref.shape, NEG, F32)
    l_ref[...] = jnp.zeros(l_ref.shape, F32)
    acc_ref[...] = jnp.zeros(acc_ref.shape, F32)

    def sel_tile(k0, causal):
        blk_of_key = (k0 + lax.broadcasted_iota(jnp.int32, (tk, HEAD_DIM), 0)) // BLK_SEL
        one_hot = (blk_of_key == lax.broadcasted_iota(jnp.int32, (tk, HEAD_DIM), 1)).astype(BF16)
        keys = jnp.concatenate([ks_ref[pl.ds(k0, tk), :], one_hot], axis=1)
        s = jnp.dot(keys, qt_sel, preferred_element_type=F32)
        if causal:
            kpos = k0 + lax.broadcasted_iota(jnp.int32, (tk, 1), 0)
            s = s + per_head(jnp.where(kpos <= qpos, 0.0, NEG))
        m_old = m_ref[...]
        m_new = jnp.maximum(m_old, jnp.max(s, axis=0, keepdims=True))
        a = jnp.exp2(m_old - m_new)
        p = jnp.exp2(s - m_new)
        l_ref[...] = a * l_ref[...] + jnp.sum(p, axis=0, keepdims=True)
        pv = jnp.dot(vst_ref[:, pl.ds(k0, tk)], p.astype(BF16), preferred_element_type=F32)
        acc_ref[...] = a * acc_ref[...] + pv
        m_ref[...] = m_new

    n_tiles = (q0 + tq + tk - 1) // tk

    def full_tile(kj, carry):
        sel_tile(pl.multiple_of(kj * tk, tk), causal=False)
        return carry

    lax.fori_loop(0, n_tiles - 1, full_tile, 0)
    sel_tile(pl.multiple_of((n_tiles - 1) * tk, tk), causal=True)
    o_sel = acc_ref[...] / l_ref[...]

    band = min(WINDOW + tq, t_len)
    kstart = pl.multiple_of(jnp.clip(q0 - WINDOW, 0, t_len - band), 128)
    dist = qpos - (kstart + lax.broadcasted_iota(jnp.int32, (band, 1), 0))
    bias_w = jnp.where((dist >= 0) & (dist < WINDOW), 0.0, NEG)
    s = jnp.dot(kw_ref[pl.ds(kstart, band), :], qt, preferred_element_type=F32) + per_head(bias_w)
    e = jnp.exp2(s - jnp.max(s, axis=0, keepdims=True))
    pv = jnp.dot(vwt_ref[:, pl.ds(kstart, band)], e.astype(BF16), preferred_element_type=F32)
    o_win = pv / jnp.sum(e, axis=0, keepdims=True)

    gates_t = jax.nn.sigmoid(gate_ref[...]).T
    for g in range(gqa):
        out_t = (gates_t[3 * g:3 * g + 1] * head(o_cmp, g) + gates_t[3 * g + 1:3 * g + 2] * head(o_sel, g)
                 + gates_t[3 * g + 2:3 * g + 3] * head(o_win, g))
        o_ref[:, g * HEAD_DIM:(g + 1) * HEAD_DIM] = out_t.T.astype(o_ref.dtype)


def _attn_prompt(qt, k_cmp, vt_cmp, sel_k, sel_vt, win_k, win_vt, gates, *, bsz, t_len, gqa, tq, tk=512):
    tk = _pick(t_len, tk)
    assert tk % tq == 0 and -(-t_len // BLK_SEL) <= HEAD_DIM
    nt = t_len // tq
    nc = k_cmp.shape[2]
    n_cmp = (t_len - BLK_CMP) // STRIDE_CMP + 1
    n_sel = -(-t_len // BLK_SEL)
    per_head = lambda r, c: pl.BlockSpec((None, None, r, c), lambda b, h, i: (b, h, 0, 0))
    kspec = pl.BlockSpec((t_len, HEAD_DIM), lambda b, h, i: (b, h))
    return pl.pallas_call(
        functools.partial(_attn_prompt_kernel, tq=tq, tk=tk, t_len=t_len, gqa=gqa, n_cmp=n_cmp, n_sel=n_sel),
        grid=(bsz, N_KV, nt),
        in_specs=[pl.BlockSpec((None, None, HEAD_DIM, gqa * tq), lambda b, h, i: (b, h, 0, i)),
                  per_head(nc, HEAD_DIM), per_head(HEAD_DIM, nc),
                  kspec, per_head(HEAD_DIM, t_len), kspec, per_head(HEAD_DIM, t_len),
                  pl.BlockSpec((tq, HEAD_DIM), lambda b, h, i: (b * nt + i, h))],
        out_specs=pl.BlockSpec((tq, gqa * HEAD_DIM), lambda b, h, i: (b * nt + i, h)),
        out_shape=jax.ShapeDtypeStruct((bsz * t_len, N_KV * gqa * HEAD_DIM), BF16),
        scratch_shapes=[pltpu.VMEM((1, gqa * tq), F32), pltpu.VMEM((1, gqa * tq), F32),
                        pltpu.VMEM((HEAD_DIM, gqa * tq), F32)],
        compiler_params=_cparams(("parallel", "parallel", "arbitrary")),
        name="attn_prompt",
    )(qt, k_cmp, vt_cmp, sel_k, sel_vt, win_k, win_vt, gates)


def _attn_sample_a_kernel(q_ref, kc_ref, vct_ref, win_ref, nw_ref, ocmp_ref, owin_ref, idx_ref,
                          *, gqa, n_cmp, n_sel, qpos, wb):
    h = pl.program_id(1)
    qs = q_ref[...]
    nc = kc_ref.shape[0]

    n = lax.broadcasted_iota(jnp.int32, (1, nc), 1)
    ok = (n * STRIDE_CMP + BLK_CMP - 1 <= qpos) & (n < n_cmp)
    s = jnp.where(ok, _dot_nt(qs, kc_ref[...]), NEG)
    e = jnp.where(ok, jnp.exp2(s - jnp.max(s, axis=-1, keepdims=True)), 0.0)
    p_cmp = e / jnp.maximum(jnp.sum(e, axis=-1, keepdims=True), 1e-30)
    ocmp_ref[...] = _dot_nt(p_cmp.astype(BF16), vct_ref[...])

    psum = jnp.sum(p_cmp, axis=0, keepdims=True)
    nsl = 128 * pl.cdiv(n_sel, 128)
    imp = _dot_exact01(jnp.broadcast_to(psum, (8, nc)), _overlap(nc, nsl, n_sel, cmp_axis=0))[0:1]
    score = _select_scores(imp, jnp.full((1, 1), qpos, jnp.int32), n_sel, axis=1)
    cnt = _rank_counts(score, n_sel, axis=1)
    lane = lax.broadcasted_iota(jnp.int32, (1, nsl), 1)
    slot = lax.broadcasted_iota(jnp.int32, (N_SEL, HEAD_DIM), 0)
    idx = jnp.zeros((N_SEL, HEAD_DIM), jnp.int32)
    for k in range(N_SEL):
        blk = jnp.sum(jnp.where(cnt == k, lane, 0), axis=-1, keepdims=True)
        idx = jnp.where(slot == k, blk, idx)
    idx_ref[...] = idx

    width = 2 * N_KV
    rows = win_ref[...].reshape(wb * width, HEAD_DIM).astype(BF16)
    kn = nw_ref[pl.ds(h, 1), :]
    vn = nw_ref[pl.ds(N_KV + h, 1), :]
    col = lax.broadcasted_iota(jnp.int32, (1, wb * width), 1)
    kp = qpos - wb + col // width
    okb = (col % width == h) & (qpos - kp < WINDOW) & (kp >= 0)
    sb = jnp.where(okb, _dot_nt(qs, rows), NEG)
    sn = jnp.sum(qs.astype(F32) * kn, axis=-1, keepdims=True)
    mx = jnp.maximum(jnp.max(sb, axis=-1, keepdims=True), sn)
    pb = jnp.where(okb, jnp.exp2(sb - mx), 0.0)
    pn = jnp.exp2(sn - mx)
    den = jnp.sum(pb, axis=-1, keepdims=True) + pn
    num = jnp.dot(pltpu.roll(pb, N_KV, 1).astype(BF16), rows, preferred_element_type=F32) + pn * vn
    owin_ref[...] = num / den


def _attn_sample_a(q4, k_cmp, vt_cmp, win_all, new_win8, layer, *, gqa, qpos):
    db = q4.shape[0]
    nc = k_cmp.shape[2]
    wb = win_all.shape[1]
    n_cmp = (qpos + 1 - BLK_CMP) // STRIDE_CMP + 1
    n_sel = -(-(qpos + 1) // BLK_SEL)
    head = pl.BlockSpec((None, None, gqa, HEAD_DIM), lambda b, h: (b, h, 0, 0))
    return pl.pallas_call(
        functools.partial(_attn_sample_a_kernel, gqa=gqa, n_cmp=n_cmp, n_sel=n_sel, qpos=qpos, wb=wb),
        grid=(db, N_KV),
        in_specs=[head, pl.BlockSpec((None, None, nc, HEAD_DIM), lambda b, h: (b, h, 0, 0)),
                  pl.BlockSpec((None, None, HEAD_DIM, nc), lambda b, h: (b, h, 0, 0)),
                  pl.BlockSpec((None, wb, 2 * N_KV, HEAD_DIM), lambda b, h: (layer * db + b, 0, 0, 0)),
                  pl.BlockSpec((None, 2 * N_KV, HEAD_DIM), lambda b, h: (b, 0, 0))],
        out_specs=[head, head, pl.BlockSpec((None, None, N_SEL, HEAD_DIM), lambda b, h: (b, h, 0, 0))],
        out_shape=[jax.ShapeDtypeStruct((db, N_KV, gqa, HEAD_DIM), F32),
                   jax.ShapeDtypeStruct((db, N_KV, gqa, HEAD_DIM), F32),
                   jax.ShapeDtypeStruct((db, N_KV, N_SEL, HEAD_DIM), jnp.int32)],
        compiler_params=_cparams(("parallel", "parallel")),
        name="attn_sample_cmp_win",
    )(q4, k_cmp, vt_cmp, win_all, new_win8)


def _attn_sample_b_kernel(idx_ref, phys_ref, *refs, gqa, qpos, n_past_blocks):
    del phys_ref
    q_ref = refs[0]
    blk_refs = refs[1:1 + N_SEL]
    ns_ref, ocmp_ref, owin_ref, gate_ref, o_ref = refs[1 + N_SEL:]
    b, h = pl.program_id(0), pl.program_id(1)
    qs = q_ref[...]
    width = 2 * N_KV
    ncol = BLK_SEL * width
    col = lax.broadcasted_iota(jnp.int32, (1, ncol), 1)

    kn = ns_ref[pl.ds(h, 1), :]
    vn = ns_ref[pl.ds(N_KV + h, 1), :]
    sn = jnp.sum(qs.astype(F32) * kn, axis=-1, keepdims=True)

    scores, mx = [], sn
    for k in range(N_SEL):
        blk = idx_ref[(b * N_KV + h) * N_SEL + k]
        ok = (col % width == h) & (blk * BLK_SEL + col // width <= qpos) & (blk < n_past_blocks)
        s = jnp.where(ok, _dot_nt(qs, blk_refs[k][...].reshape(ncol, HEAD_DIM).astype(BF16)), NEG)
        scores.append(s)
        mx = jnp.maximum(mx, jnp.max(s, axis=-1, keepdims=True))
    pn = jnp.exp2(sn - mx)
    den, num = pn, pn * vn
    for k in range(N_SEL):
        p = jnp.exp2(scores[k] - mx)
        den = den + jnp.sum(p, axis=-1, keepdims=True)
        num = num + jnp.dot(pltpu.roll(p, N_KV, 1).astype(BF16),
                            blk_refs[k][...].reshape(ncol, HEAD_DIM).astype(BF16), preferred_element_type=F32)
    o_sel = num / den

    gates = jax.nn.sigmoid(gate_ref[...])
    o_cmp, o_win = ocmp_ref[...], owin_ref[...]
    rows = []
    for g in range(gqa):
        rows.append(gates[:, 3 * g:3 * g + 1] * o_cmp[g:g + 1] + gates[:, 3 * g + 1:3 * g + 2] * o_sel[g:g + 1]
                    + gates[:, 3 * g + 2:3 * g + 3] * o_win[g:g + 1])
    o_ref[...] = jnp.concatenate(rows, axis=0)


def _attn_sample_b(idx, phys, q4, sel_blocks, new_sel8, o_cmp, o_win, gates4, *, gqa, qpos, n_past_blocks):
    db = q4.shape[0]
    head = pl.BlockSpec((None, None, gqa, HEAD_DIM), lambda b, h, i, p: (b, h, 0, 0))

    def blk_spec(k):
        return pl.BlockSpec((None, BLK_SEL, 2 * N_KV, HEAD_DIM),
                            lambda b, h, i, p: (p[(b * N_KV + h) * N_SEL + k], 0, 0, 0))

    return pl.pallas_call(
        functools.partial(_attn_sample_b_kernel, gqa=gqa, qpos=qpos, n_past_blocks=n_past_blocks),
        grid_spec=pltpu.PrefetchScalarGridSpec(
            num_scalar_prefetch=2,
            grid=(db, N_KV),
            in_specs=[head] + [blk_spec(k) for k in range(N_SEL)]
            + [pl.BlockSpec((None, 2 * N_KV, HEAD_DIM), lambda b, h, i, p: (b, 0, 0)),
               head, head,
               pl.BlockSpec((None, None, 1, HEAD_DIM), lambda b, h, i, p: (b, h, 0, 0))],
            out_specs=head,
        ),
        out_shape=jax.ShapeDtypeStruct((db, N_KV, gqa, HEAD_DIM), F32),
        compiler_params=_cparams(("parallel", "parallel")),
        name="attn_sample_sel",
    )(idx.reshape(-1), phys.reshape(-1), q4, *([sel_blocks] * N_SEL), new_sel8, o_cmp, o_win, gates4)


def _roll_window_kernel(s_ref, n_ref, o_ref, *, wb):
    o_ref[0:wb - 1] = s_ref[1:wb]
    o_ref[wb - 1] = n_ref[...]


def _roll_window(state, new_rows):
    n, wb = state.shape[:2]
    tile = (2 * N_KV, HEAD_DIM)
    return pl.pallas_call(
        functools.partial(_roll_window_kernel, wb=wb),
        grid=(n,),
        in_specs=[pl.BlockSpec((None, wb) + tile, lambda b: (b, 0, 0, 0)), pl.BlockSpec((None,) + tile, lambda b: (b, 0, 0))],
        out_specs=pl.BlockSpec((None, wb) + tile, lambda b: (b, 0, 0, 0)),
        out_shape=jax.ShapeDtypeStruct(state.shape, state.dtype),
        compiler_params=_cparams(("parallel",)),
        name="roll_window",
    )(state, new_rows)


def _pad_rows(a, rows):
    return jnp.pad(a, ((0, rows - a.shape[0]), (0, 0)))


def kernel(x_prompt, x_sample, cache_cmp, cache_sel, state_win, state_conv, page_table, w_in, conv_w, cmp_pe, cmp_w1, cmp_w2, w_out, ln1_g, ln1_b, w_up, w_down, ln2_g, ln2_b):
    bsz, t_len, d = x_prompt.shape
    db, dec_seq, _ = x_sample.shape
    depth = w_in.shape[0]
    cd = state_conv.shape[-1]
    ad = d - cd
    gqa = ad // HEAD_DIM // N_KV
    pool, page_rows = cache_cmp.shape[1], cache_cmp.shape[2]
    n_pages = page_table.shape[1]
    past = n_pages * page_rows
    wb = state_win.shape[2]
    assert dec_seq == 1 and wb == WINDOW and t_len >= WINDOW and 3 * gqa <= HEAD_DIM and db <= SAMPLE_ROWS
    assert page_rows % BLK_SEL == 0 and t_len % page_rows == 0
    alpha = (2.0 * depth) ** 0.25
    nmain = 3 * cd + ad + 6 * KV_DIM
    cpp = page_rows // STRIDE_CMP
    bpp = page_rows // BLK_SEL
    tile = (2 * N_KV, HEAD_DIM)
    tq = _pick(t_len, ATTN_TQ)

    tab_p = _rope_tables(jnp.arange(t_len, dtype=jnp.int32))
    tab_s = _rope_tables(jnp.full((db,), past, jnp.int32))
    pt_prompt = jnp.arange(bsz * (t_len // page_rows), dtype=jnp.int32).reshape(bsz, t_len // page_rows)

    wg = w_in[:, :, nmain:].reshape(depth, d, N_KV, 3 * gqa)
    wg = jnp.pad(wg, ((0, 0), (0, 0), (0, 0), (0, HEAD_DIM - 3 * gqa))).reshape(depth, d, KV_DIM).astype(BF16)
    half = STRIDE_CMP * HEAD_DIM
    w1_flat = cmp_w1.reshape(depth, 2, BLK_CMP * HEAD_DIM, HEAD_DIM).astype(BF16)
    w12 = jnp.concatenate([w1_flat[:, :, :half], w1_flat[:, :, half:]], axis=3)
    pe_flat = jnp.broadcast_to(cmp_pe.reshape(depth, 2, 1, BLK_CMP * HEAD_DIM), (depth, 2, 8, BLK_CMP * HEAD_DIM)).astype(BF16)
    w2 = cmp_w2.astype(BF16)
    w_down_bf = w_down.astype(BF16)
    w_in_nk = jnp.swapaxes(w_in, 1, 2)
    ln = [a.reshape(depth, 1, d) for a in (ln1_g, ln1_b, ln2_g, ln2_b)]

    cmp_pages = cache_cmp.reshape(depth * pool, cpp, STRIDE_CMP, *tile)
    sel_blocks = cache_sel.reshape(depth * pool * bpp, BLK_SEL, *tile)
    win_all = state_win.reshape(depth * db, wb, *tile)

    xp = x_prompt.reshape(bsz * t_len, d)
    xp_bf = xp.astype(BF16)
    xs = _pad_rows(x_sample.reshape(db, d), SAMPLE_ROWS)
    xs_bf = xs.astype(BF16)

    outs = {k: [] for k in ("cmp_p", "sel_p", "win_p", "conv_p", "cmp_s", "sel_s", "win_s", "conv_s")}
    for l in range(depth):
        proj, proj_s = _matmul_ws([xp_bf], [xs_bf], w_in_nk, l, n_out=nmain, out_dtype=F32, w_is_nk=True, name="mm_in")
        gates = _matmul(xp_bf, wg, l, out_dtype=F32, bn=KV_DIM, name="mm_gate")
        gates_s = _matmul(xs_bf, wg, l, out_dtype=F32, bn=KV_DIM, name="mm_gate_s")[:db]

        yconv, qt, new_cmp, new_sel, new_win, sel_k, sel_vt, win_k, win_vt, new_conv = _prep_prompt(
            proj, tab_p, conv_w, l, bsz=bsz, t_len=t_len, cd=cd, ad=ad, tt=tq)
        p12 = _compress_chunks(new_cmp.reshape(-1, cpp, STRIDE_CMP, *tile), pt_prompt, w12, l)
        k_cmp, vt_cmp = _compress_finish(p12, pe_flat, w1_flat, w2, l)
        o_attn = _attn_prompt(qt, k_cmp, vt_cmp, sel_k, sel_vt, win_k, win_vt, gates,
                              bsz=bsz, t_len=t_len, gqa=gqa, tq=tq)

        yconv_s, q_s, cmp_row, sel_row, win_row, u_s = _prep_sample(
            proj_s[:db], tab_s, conv_w[l], state_conv[l, :, 0], state_conv[l, :, 1], cd=cd, ad=ad)
        p12_s = _compress_chunks(cmp_pages, page_table + l * pool, w12, l)
        k_cmp_s, vt_cmp_s = _compress_finish(p12_s, pe_flat, w1_flat, w2, l)
        q4 = q_s.reshape(db, N_KV, gqa, HEAD_DIM)
        o_cmp_s, o_win_s, idx = _attn_sample_a(q4, k_cmp_s, vt_cmp_s, win_all, win_row.reshape(db, *tile), l,
                                               gqa=gqa, qpos=past)
        idx = idx[..., 0]
        page = jnp.minimum(idx // bpp, n_pages - 1).reshape(db, N_KV * N_SEL)
        phys = (jnp.take_along_axis(page_table, page, axis=1) + l * pool) * bpp + idx.reshape(db, -1) % bpp
        o_attn_s = _attn_sample_b(idx, phys, q4, sel_blocks, sel_row.reshape(db, *tile), o_cmp_s, o_win_s,
                                  gates_s.reshape(db, N_KV, 1, HEAD_DIM), gqa=gqa, qpos=past, n_past_blocks=n_pages * bpp)
        yconv_s = _pad_rows(yconv_s, SAMPLE_ROWS).astype(BF16)
        o_attn_s = _pad_rows(o_attn_s.reshape(db, ad), SAMPLE_ROWS).astype(BF16)

        h1, h1_s = _matmul_ws([yconv, o_attn], [yconv_s, o_attn_s], w_out, l, n_out=d, out_dtype=F32,
                              res=xp, res_s=xs, alpha=alpha, name="mm_out")
        x1, x1_bf = _layer_norm(h1, ln[0], ln[1], l)
        x1_s, x1_s_bf = _layer_norm(h1_s, ln[0], ln[1], l)
        up, up_s = _matmul_ws([x1_bf], [x1_s_bf], w_up, l, n_out=w_up.shape[2], out_dtype=BF16, act="relu2", name="mm_up")
        h2 = _matmul(up, w_down_bf, l, out_dtype=F32, res=x1, alpha=alpha, bk=2048, name="mm_down")
        h2_s = _matmul(up_s, w_down_bf, l, out_dtype=F32, res=x1_s, alpha=alpha, bk=2048, name="mm_down_s")
        xp, xp_bf = _layer_norm(h2, ln[2], ln[3], l)
        xs, xs_bf = _layer_norm(h2_s, ln[2], ln[3], l)

        kv_shape = (2, N_KV, HEAD_DIM)
        outs["cmp_p"].append(new_cmp.reshape(bsz, t_len, *kv_shape))
        outs["sel_p"].append(new_sel.reshape(bsz, t_len, *kv_shape))
        outs["win_p"].append(new_win.reshape(bsz, t_len, *kv_shape)[:, t_len - min(WINDOW, t_len):])
        outs["conv_p"].append(new_conv)
        outs["cmp_s"].append(cmp_row.reshape(db, 1, *kv_shape))
        outs["sel_s"].append(sel_row.reshape(db, 1, *kv_shape))
        outs["win_s"].append(win_row.reshape(db, *tile))
        outs["conv_s"].append(jnp.stack([state_conv[l, :, 1], u_s], axis=1))

    st = {k: jnp.stack(v) for k, v in outs.items()}
    new_win_s = _roll_window(win_all, st["win_s"].reshape(depth * db, *tile)).reshape(depth, db, wb, 2, N_KV, HEAD_DIM)
    return (xp.reshape(bsz, t_len, d), xs[:db].reshape(db, 1, d), st["cmp_p"], st["sel_p"], st["win_p"], st["conv_p"],
            st["cmp_s"], st["sel_s"], new_win_s, st["conv_s"])
```

```python
import functools
import math

import jax
import jax.numpy as jnp
from jax import lax
from jax.experimental import pallas as pl
from jax.experimental.pallas import tpu as pltpu

F32 = jnp.float32
BF16 = jnp.bfloat16

HEAD_DIM = 128
N_KV = 4
KV_DIM = N_KV * HEAD_DIM
ROT_DIM = HEAD_DIM // 4
ROPE_THETA = 500000.0
BLK_CMP = 32
STRIDE_CMP = 16
BLK_SEL = 64
N_SEL = 16
WINDOW = 512
CONV_W = 3
LN_EPS = 1e-5
FORCE = 1e4
NEG = -1e30
Q_SCALE = HEAD_DIM ** -0.5 * math.log2(math.e)
SAMPLE_ROWS = 16
ATTN_TQ = 256
CAST_CHUNK = 512

VMEM_LIMIT_BYTES = 60 * 1024 * 1024


def _cparams(sem):
    return pltpu.CompilerParams(dimension_semantics=sem, vmem_limit_bytes=VMEM_LIMIT_BYTES)


def _pick(n, pref):
    if n <= pref:
        return n
    t = pref
    while n % t:
        t //= 2
    return t


def _epilogue(acc, res, *, act, alpha):
    if act == "relu2":
        acc = jnp.square(jnp.maximum(acc, 0.0))
    if res is not None:
        acc = alpha * res + acc
    return acc


def _mm_ws_kernel(*refs, n_x, ksplit, act, alpha, has_res, w_is_nk):
    w_ref = refs[0]
    x_refs = refs[1:1 + n_x]
    xs_refs = refs[1 + n_x:1 + 2 * n_x]
    pos = 1 + 2 * n_x
    res_ref, ress_ref = (refs[pos], refs[pos + 1]) if has_res else (None, None)
    pos += 2 * has_res
    o_ref, os_ref, wbf_ref = refs[pos], refs[pos + 1], refs[pos + 2]

    def product(x_list):
        acc = None
        for x_ref, (k0, k1) in zip(x_list, ksplit):
            part = jnp.dot(x_ref[...], wbf_ref[k0:k1, :], preferred_element_type=F32)
            acc = part if acc is None else acc + part
        return acc

    @pl.when(pl.program_id(1) == 0)
    def _():
        kdim = wbf_ref.shape[0]
        ck = math.gcd(kdim, CAST_CHUNK)
        for k0 in range(0, kdim, ck):
            if w_is_nk:
                wbf_ref[k0:k0 + ck, :] = w_ref[:, k0:k0 + ck].T.astype(BF16)
            else:
                wbf_ref[k0:k0 + ck, :] = w_ref[k0:k0 + ck, :].astype(BF16)
        res_s = ress_ref[...] if has_res else None
        os_ref[...] = _epilogue(product(xs_refs), res_s, act=act, alpha=alpha).astype(os_ref.dtype)

    res = res_ref[...] if has_res else None
    o_ref[...] = _epilogue(product(x_refs), res, act=act, alpha=alpha).astype(o_ref.dtype)


def _matmul_ws(xs, xs_s, w, layer, *, n_out, out_dtype, act=None, res=None, res_s=None, alpha=1.0,
               w_is_nk=False, w_single_buffer=False, bm=1024, bn=512, name="mm"):
    m = xs[0].shape[0]
    ms = xs_s[0].shape[0]
    kdim = w.shape[2] if w_is_nk else w.shape[1]
    bm, bn = _pick(m, bm), _pick(n_out, bn)
    ksplit, k0 = [], 0
    for x in xs:
        ksplit.append((k0, k0 + x.shape[1]))
        k0 += x.shape[1]
    assert k0 == kdim
    w_mode = dict(pipeline_mode=pl.Buffered(1)) if w_single_buffer else {}
    if w_is_nk:
        in_specs = [pl.BlockSpec((None, bn, kdim), lambda j, i: (layer, j, 0), **w_mode)]
    else:
        in_specs = [pl.BlockSpec((None, kdim, bn), lambda j, i: (layer, 0, j), **w_mode)]
    in_specs += [pl.BlockSpec((bm, x.shape[1]), lambda j, i: (i, 0)) for x in xs]
    in_specs += [pl.BlockSpec((ms, x.shape[1]), lambda j, i: (0, 0)) for x in xs_s]
    args = [w, *xs, *xs_s]
    if res is not None:
        in_specs += [pl.BlockSpec((bm, bn), lambda j, i: (i, j)), pl.BlockSpec((ms, bn), lambda j, i: (0, j))]
        args += [res, res_s]
    return pl.pallas_call(
        functools.partial(_mm_ws_kernel, n_x=len(xs), ksplit=tuple(ksplit), act=act, alpha=alpha,
                          has_res=res is not None, w_is_nk=w_is_nk),
        grid=(n_out // bn, m // bm),
        in_specs=in_specs,
        out_specs=[pl.BlockSpec((bm, bn), lambda j, i: (i, j)), pl.BlockSpec((ms, bn), lambda j, i: (0, j))],
        out_shape=[jax.ShapeDtypeStruct((m, n_out), out_dtype), jax.ShapeDtypeStruct((ms, n_out), out_dtype)],
        scratch_shapes=[pltpu.VMEM((kdim, bn), BF16)],
        compiler_params=_cparams(("arbitrary", "arbitrary")),
        name=name,
    )(*args)


def _mm_kernel(*refs, nk, act, alpha, has_res):
    x_ref, w_ref = refs[0], refs[1]
    res_ref = refs[2] if has_res else None
    o_ref = refs[2 + has_res]
    acc_ref = (refs[3 + has_res] if len(refs) > 3 + has_res else o_ref) if nk > 1 else None
    part = jnp.dot(x_ref[...], w_ref[...], preferred_element_type=F32)

    def finish(acc):
        res = res_ref[...] if has_res else None
        o_ref[...] = _epilogue(acc, res, act=act, alpha=alpha).astype(o_ref.dtype)

    if nk == 1:
        finish(part)
    else:
        k = pl.program_id(2)

        @pl.when(k == 0)
        def _():
            acc_ref[...] = part

        @pl.when((k > 0) & (k < nk - 1))
        def _():
            acc_ref[...] += part

        @pl.when(k == nk - 1)
        def _():
            finish(acc_ref[...] + part)


def _matmul(x, w, layer=None, *, out_dtype, act=None, res=None, alpha=1.0, bm=1024, bn=1024, bk=4096, name="mm"):
    m, kdim = x.shape
    n = w.shape[-1]
    bm, bn, bk = _pick(m, bm), _pick(n, bn), _pick(kdim, bk)
    nk = kdim // bk
    if layer is None:
        w_spec = pl.BlockSpec((bk, bn), lambda i, j, k: (k, j))
    else:
        w_spec = pl.BlockSpec((None, bk, bn), lambda i, j, k: (layer, k, j))
    in_specs = [pl.BlockSpec((bm, bk), lambda i, j, k: (i, k)), w_spec]
    args = [x, w]
    if res is not None:
        in_specs.append(pl.BlockSpec((bm, bn), lambda i, j, k: (i, j)))
        args.append(res)
    return pl.pallas_call(
        functools.partial(_mm_kernel, nk=nk, act=act, alpha=alpha, has_res=res is not None),
        grid=(m // bm, n // bn, nk),
        in_specs=in_specs,
        out_specs=pl.BlockSpec((bm, bn), lambda i, j, k: (i, j)),
        out_shape=jax.ShapeDtypeStruct((m, n), out_dtype),
        scratch_shapes=[pltpu.VMEM((bm, bn), F32)] if nk > 1 and out_dtype != F32 else [],
        compiler_params=_cparams(("parallel", "parallel", "arbitrary")),
        name=name,
    )(*args)


def _ln_kernel(h_ref, g_ref, b_ref, o_ref, obf_ref):
    h = h_ref[...]
    mu = jnp.mean(h, axis=-1, keepdims=True)
    d = h - mu
    var = jnp.mean(d * d, axis=-1, keepdims=True)
    y = d * lax.rsqrt(var + LN_EPS) * g_ref[...] + b_ref[...]
    o_ref[...] = y
    obf_ref[...] = y.astype(BF16)


def _layer_norm(h, g, b, layer, *, bm=256):
    m, d = h.shape
    bm = _pick(m, bm)
    row = pl.BlockSpec((bm, d), lambda i: (i, 0))
    vec = pl.BlockSpec((None, 1, d), lambda i: (layer, 0, 0))
    return pl.pallas_call(
        _ln_kernel,
        grid=(m // bm,),
        in_specs=[row, vec, vec],
        out_specs=[row, row],
        out_shape=[jax.ShapeDtypeStruct((m, d), F32), jax.ShapeDtypeStruct((m, d), BF16)],
        compiler_params=_cparams(("parallel",)),
        name="layer_norm",
    )(h, g, b)


def _rope_tables(pos):
    half = ROT_DIM // 2
    inv = ROPE_THETA ** (-jnp.arange(half, dtype=F32) / half)
    ang = pos.astype(F32)[:, None] * inv[None, :]
    cos, sin = jnp.cos(ang), jnp.sin(ang)
    n = pos.shape[0]
    one = jnp.ones((n, HEAD_DIM - ROT_DIM), F32)
    zero = jnp.zeros((n, HEAD_DIM - ROT_DIM), F32)
    zh = jnp.zeros((n, half), F32)
    c = jnp.concatenate([cos, cos, one], 1)
    s1 = jnp.concatenate([zh, sin, zero], 1)
    s2 = jnp.concatenate([-sin, zh, zero], 1)
    return c, s1, s2


def _rope(x, c, s1, s2):
    return x * c + pltpu.roll(x, ROT_DIM // 2, 1) * s1 + pltpu.roll(x, HEAD_DIM - ROT_DIM // 2, 1) * s2


def _proj_offsets(cd, ad):
    offs = [0, cd, 2 * cd, 3 * cd, 3 * cd + ad]
    for _ in range(5):
        offs.append(offs[-1] + KV_DIM)
    return offs


def _prep_prompt_kernel(proj_ref, c_ref, s1_ref, s2_ref, cw_ref,
                        yconv_ref, qt_ref, cmp_ref, sel_ref, win_ref, selk_ref, selvt_ref, wink_ref, winvt_ref,
                        nconv_ref, carry_ref, *, cd, ad, tt):
    t = pl.program_id(1)
    c, s1, s2 = c_ref[...], s1_ref[...], s2_ref[...]
    offs = _proj_offsets(cd, ad)

    @pl.when(t == 0)
    def _():
        carry_ref[...] = jnp.zeros_like(carry_ref)

    gb = proj_ref[:, offs[0]:offs[0] + cd]
    u = proj_ref[:, offs[1]:offs[1] + cd] * proj_ref[:, offs[2]:offs[2] + cd]
    row = lax.broadcasted_iota(jnp.int32, (tt, cd), 0)
    c0 = carry_ref[0:1, :]
    c1 = carry_ref[1:2, :]
    u1 = jnp.where(row == 0, c1, pltpu.roll(u, 1, 0))
    u2 = jnp.where(row == 0, c0, jnp.where(row == 1, c1, pltpu.roll(u, 2, 0)))
    y = cw_ref[0:1, :] * u2 + cw_ref[1:2, :] * u1 + cw_ref[2:3, :] * u
    yconv_ref[...] = (gb * y).astype(BF16)
    tail = (proj_ref[tt - 2:tt, offs[1]:offs[1] + cd] * proj_ref[tt - 2:tt, offs[2]:offs[2] + cd])
    carry_ref[...] = tail
    nconv_ref[...] = tail

    gqa = ad // HEAD_DIM // N_KV
    for h in range(ad // HEAD_DIM):
        lo = offs[3] + h * HEAD_DIM
        g = h % gqa
        qt = (_rope(proj_ref[:, lo:lo + HEAD_DIM], c, s1, s2) * Q_SCALE).T
        qt_ref[h // gqa, :, g * tt:(g + 1) * tt] = qt.astype(BF16)

    for k_off, out_ref, k_ref, vt_ref in ((offs[4], cmp_ref, None, None), (offs[6], sel_ref, selk_ref, selvt_ref),
                                          (offs[8], win_ref, wink_ref, winvt_ref)):
        for h in range(N_KV):
            lo = k_off + h * HEAD_DIM
            kr = _rope(proj_ref[:, lo:lo + HEAD_DIM], c, s1, s2)
            v = proj_ref[:, lo + KV_DIM:lo + KV_DIM + HEAD_DIM]
            out_ref[:, h, :] = kr
            out_ref[:, N_KV + h, :] = v
            if k_ref is not None:
                k_ref[:, h * HEAD_DIM:(h + 1) * HEAD_DIM] = kr.astype(BF16)
                vt_ref[h] = v.T.astype(BF16)


def _prep_prompt(proj, tables, conv_w, layer, *, bsz, t_len, cd, ad, tt):
    m, nmain = proj.shape
    nt = t_len // tt
    gqa = ad // HEAD_DIM // N_KV
    rows = lambda w: pl.BlockSpec((tt, w), lambda b, t: (b * nt + t, 0))
    rows3 = pl.BlockSpec((tt, 2 * N_KV, HEAD_DIM), lambda b, t: (b * nt + t, 0, 0))
    tab = pl.BlockSpec((tt, HEAD_DIM), lambda b, t: (t, 0))
    cols = lambda w: pl.BlockSpec((None, N_KV, HEAD_DIM, w), lambda b, t: (b, 0, 0, t))
    kv_rows = jax.ShapeDtypeStruct((m, 2 * N_KV, HEAD_DIM), F32)
    k_rows = jax.ShapeDtypeStruct((m, KV_DIM), BF16)
    v_cols = jax.ShapeDtypeStruct((bsz, N_KV, HEAD_DIM, t_len), BF16)
    return pl.pallas_call(
        functools.partial(_prep_prompt_kernel, cd=cd, ad=ad, tt=tt),
        grid=(bsz, nt),
        in_specs=[rows(nmain), tab, tab, tab, pl.BlockSpec((None, CONV_W, cd), lambda b, t: (layer, 0, 0))],
        out_specs=[rows(cd), cols(gqa * tt), rows3, rows3, rows3, rows(KV_DIM), cols(tt), rows(KV_DIM), cols(tt),
                   pl.BlockSpec((None, CONV_W - 1, cd), lambda b, t: (b, 0, 0))],
        out_shape=[jax.ShapeDtypeStruct((m, cd), BF16), jax.ShapeDtypeStruct((bsz, N_KV, HEAD_DIM, gqa * t_len), BF16),
                   kv_rows, kv_rows, kv_rows, k_rows, v_cols, k_rows, v_cols,
                   jax.ShapeDtypeStruct((bsz, CONV_W - 1, cd), F32)],
        scratch_shapes=[pltpu.VMEM((CONV_W - 1, cd), F32)],
        compiler_params=_cparams(("parallel", "arbitrary")),
        name="prep_prompt",
    )(proj, *tables, conv_w)


def _prep_sample_kernel(proj_ref, c_ref, s1_ref, s2_ref, cw_ref, b0_ref, b1_ref,
                        yconv_ref, q_ref, cmp_ref, sel_ref, win_ref, u_ref, *, cd, ad):
    c, s1, s2 = c_ref[...], s1_ref[...], s2_ref[...]
    offs = _proj_offsets(cd, ad)
    gb = proj_ref[:, offs[0]:offs[0] + cd]
    u = proj_ref[:, offs[1]:offs[1] + cd] * proj_ref[:, offs[2]:offs[2] + cd]
    y = cw_ref[0:1, :] * b0_ref[...] + cw_ref[1:2, :] * b1_ref[...] + cw_ref[2:3, :] * u
    yconv_ref[...] = gb * y
    u_ref[...] = u
    for h in range(ad // HEAD_DIM):
        lo = offs[3] + h * HEAD_DIM
        q_ref[:, h * HEAD_DIM:(h + 1) * HEAD_DIM] = (_rope(proj_ref[:, lo:lo + HEAD_DIM], c, s1, s2) * Q_SCALE).astype(BF16)
    for k_off, out_ref in ((offs[4], cmp_ref), (offs[6], sel_ref), (offs[8], win_ref)):
        for h in range(N_KV):
            lo = k_off + h * HEAD_DIM
            out_ref[:, h * HEAD_DIM:(h + 1) * HEAD_DIM] = _rope(proj_ref[:, lo:lo + HEAD_DIM], c, s1, s2)
            out_ref[:, KV_DIM + h * HEAD_DIM:KV_DIM + (h + 1) * HEAD_DIM] = proj_ref[:, lo + KV_DIM:lo + KV_DIM + HEAD_DIM]


def _prep_sample(proj, tables, conv_w, buf0, buf1, *, cd, ad):
    db = proj.shape[0]
    kv2 = 2 * KV_DIM
    full = lambda a: pl.BlockSpec(a.shape, lambda i: (0,) * a.ndim)
    ins = [proj, *tables, conv_w, buf0, buf1]
    outs = [jax.ShapeDtypeStruct((db, cd), F32), jax.ShapeDtypeStruct((db, ad), BF16),
            jax.ShapeDtypeStruct((db, kv2), F32), jax.ShapeDtypeStruct((db, kv2), F32),
            jax.ShapeDtypeStruct((db, kv2), F32), jax.ShapeDtypeStruct((db, cd), F32)]
    return pl.pallas_call(
        functools.partial(_prep_sample_kernel, cd=cd, ad=ad),
        grid=(1,),
        in_specs=[full(a) for a in ins],
        out_specs=[full(o) for o in outs],
        out_shape=outs,
        compiler_params=_cparams(("arbitrary",)),
        name="prep_sample",
    )(*ins)


def _cmp12_kernel(pt_ref, *refs, npg):
    del pt_ref
    pages, w_ref, o_ref = refs[:npg], refs[npg], refs[npg + 1]
    cpp = pages[0].shape[0]
    lhs = ([], [])
    for l in range(STRIDE_CMP):
        per_page = [pltpu.einshape("cjd->jcd", pr[:, l, :, :]) for pr in pages]
        for j in range(2):
            lhs[j].append(jnp.concatenate([xt[j * N_KV + h] for h in range(N_KV) for xt in per_page], axis=0))
    for j in range(2):
        rows = jnp.concatenate(lhs[j], axis=1).astype(BF16)
        out = jnp.dot(rows, w_ref[j], preferred_element_type=F32)
        o_ref[j] = out.reshape(N_KV, npg * cpp, 2 * HEAD_DIM)


def _compress_chunks(pages5, page_ids, w12, layer, *, npg=16):
    bsz, n_pages = page_ids.shape
    cpp = pages5.shape[1]
    assert cpp == 8 and pages5.shape[2] == STRIDE_CMP, "one (8, 128) tile holds a row of all chunks of a page"
    npg = _pick(n_pages, npg)

    def page_spec(i):
        return pl.BlockSpec((None, cpp, STRIDE_CMP, 2 * N_KV, HEAD_DIM),
                            lambda b, g, pt: (pt[b * n_pages + g * npg + i], 0, 0, 0, 0))

    return pl.pallas_call(
        functools.partial(_cmp12_kernel, npg=npg),
        grid_spec=pltpu.PrefetchScalarGridSpec(
            num_scalar_prefetch=1,
            grid=(bsz, n_pages // npg),
            in_specs=[page_spec(i) for i in range(npg)]
            + [pl.BlockSpec((None,) + w12.shape[1:], lambda b, g, pt: (layer, 0, 0, 0))],
            out_specs=pl.BlockSpec((None, 2, N_KV, npg * cpp, 2 * HEAD_DIM), lambda b, g, pt: (b, 0, 0, g, 0)),
        ),
        out_shape=jax.ShapeDtypeStruct((bsz, 2, N_KV, n_pages * cpp, 2 * HEAD_DIM), F32),
        compiler_params=_cparams(("parallel", "parallel")),
        name="compress_chunks",
    )(page_ids.reshape(-1), *([pages5] * npg), w12)


def _cmp_finish_kernel(p_ref, pe_ref, w1_ref, w2_ref, w2t_ref, k_ref, vt_ref, *, nc):
    for j in range(2):
        bias = jnp.dot(pe_ref[j], w1_ref[j], preferred_element_type=F32)[0:1]
        for h in range(N_KV):
            p = p_ref[j, h]
            pre = p[:, :HEAD_DIM] + pltpu.roll(p[:, HEAD_DIM:], nc - 1, 0) + bias
            act = jax.nn.gelu(pre).astype(BF16)
            if j == 0:
                k_ref[h] = jnp.dot(act, w2_ref[j], preferred_element_type=F32).astype(BF16)
            else:
                vt_ref[h] = _dot_nt(w2t_ref[j], act).astype(BF16)


def _compress_finish(p12, pe_flat, w1_flat, w2, layer):
    bsz, _, _, nc, _ = p12.shape
    per_layer = lambda a: pl.BlockSpec((None,) + a.shape[1:], lambda b: (layer,) + (0,) * (a.ndim - 1))
    return pl.pallas_call(
        functools.partial(_cmp_finish_kernel, nc=nc),
        grid=(bsz,),
        in_specs=[pl.BlockSpec((None, 2, N_KV, nc, 2 * HEAD_DIM), lambda b: (b, 0, 0, 0, 0)),
                  per_layer(pe_flat), per_layer(w1_flat), per_layer(w2), per_layer(w2)],
        out_specs=[pl.BlockSpec((None, N_KV, nc, HEAD_DIM), lambda b: (b, 0, 0, 0)),
                   pl.BlockSpec((None, N_KV, HEAD_DIM, nc), lambda b: (b, 0, 0, 0))],
        out_shape=[jax.ShapeDtypeStruct((bsz, N_KV, nc, HEAD_DIM), BF16),
                   jax.ShapeDtypeStruct((bsz, N_KV, HEAD_DIM, nc), BF16)],
        compiler_params=_cparams(("parallel",)),
        name="compress_finish",
    )(p12, pe_flat, w1_flat, w2, jnp.swapaxes(w2, 2, 3))


def _dot_nt(a, b):
    return lax.dot_general(a, b, (((1,), (1,)), ((), ())), preferred_element_type=F32)


def _split3(p):
    hi = p.astype(BF16)
    r1 = p - hi.astype(F32)
    mid = r1.astype(BF16)
    return hi, mid, (r1 - mid.astype(F32)).astype(BF16)


def _dot_exact01(p, onehot_bf16):
    return sum(jnp.dot(t, onehot_bf16, preferred_element_type=F32) for t in _split3(p))


def _overlap(n_cmp_pad, n_sel_pad, n_sel, *, cmp_axis):
    shape = (n_cmp_pad, n_sel_pad) if cmp_axis == 0 else (n_sel_pad, n_cmp_pad)
    n = lax.broadcasted_iota(jnp.int32, shape, cmp_axis) * STRIDE_CMP
    j = lax.broadcasted_iota(jnp.int32, shape, 1 - cmp_axis)
    ov = (n < j * BLK_SEL + BLK_SEL) & (n + BLK_CMP > j * BLK_SEL) & (j < n_sel)
    return ov.astype(BF16)


def _select_scores(imp, qpos, n_sel, *, axis):
    j = lax.broadcasted_iota(jnp.int32, imp.shape, axis)
    jq = qpos // BLK_SEL
    valid = j * BLK_SEL <= qpos
    forced = (j == 0) | (j == jq) | (j == jq - 1)
    score = jnp.where(valid, jnp.where(forced, FORCE, imp), -FORCE)
    return jnp.where(j < n_sel, score, -jnp.inf)


def _rank_counts(score, n_sel, *, axis):
    j = lax.broadcasted_iota(jnp.int32, score.shape, axis)
    cnt = jnp.zeros(score.shape, jnp.int32)
    for i in range(n_sel):
        si = score[i:i + 1, :] if axis == 0 else score[:, i:i + 1]
        beats = (si > score) | ((si == score) & (i < j))
        cnt = cnt + beats.astype(jnp.int32)
    return cnt


def _attn_prompt_kernel(qt_ref, kc_ref, vct_ref, ks_ref, vst_ref, kw_ref, vwt_ref, gate_ref, o_ref,
                        m_ref, l_ref, acc_ref, *, tq, tk, t_len, gqa, n_cmp, n_sel):
    i = pl.program_id(2)
    q0 = i * tq
    qpos = q0 + lax.broadcasted_iota(jnp.int32, (1, tq), 1)
    nc = kc_ref.shape[0]
    nsr = 8 * pl.cdiv(n_sel, 8)
    qt = qt_ref[...]

    def per_head(x):
        return jnp.concatenate([x] * gqa, axis=1)

    def head(x, g):
        return x[:, g * tq:(g + 1) * tq]

    n = lax.broadcasted_iota(jnp.int32, (nc, 1), 0)
    bias_c = jnp.where((n * STRIDE_CMP + BLK_CMP - 1 <= qpos) & (n < n_cmp), 0.0, NEG)
    has_block = (qpos >= BLK_CMP - 1).astype(F32)
    s = jnp.dot(kc_ref[...], qt, preferred_element_type=F32) + per_head(bias_c)
    e = jnp.exp2(s - jnp.max(s, axis=0, keepdims=True))
    p_cmp = e * (per_head(has_block) / jnp.sum(e, axis=0, keepdims=True))
    o_cmp = jnp.dot(vct_ref[...], p_cmp.astype(BF16), preferred_element_type=F32)

    def all_started():
        j = lax.broadcasted_iota(jnp.int32, (nsr, tq), 0)
        return jnp.where(j * BLK_SEL <= qpos, 0.0, NEG)

    def top_k():
        p_sum = sum(head(p_cmp, g) for g in range(gqa))
        ov = _overlap(nc, nsr, n_sel, cmp_axis=1)
        imp = sum(jnp.dot(ov, t, preferred_element_type=F32) for t in _split3(p_sum))
        score = _select_scores(imp, qpos, n_sel, axis=0)
        return jnp.where(_rank_counts(score, n_sel, axis=0) < N_SEL, 0.0, NEG)

    sel_bias = lax.cond((q0 + tq - 1) // BLK_SEL + 1 <= N_SEL, all_started, top_k).astype(BF16)

    sel_bias = jnp.concatenate([sel_bias, jnp.zeros((HEAD_DIM - nsr, tq), BF16)], axis=0)
    qt_sel = jnp.concatenate([qt, per_head(sel_bias)], axis=0)

    m_ref[...] = jnp.full(m_ref.shape, NEG, F32)
    l_ref[...] = jnp.zeros(l_ref.shape, F32)
    acc_ref[...] = jnp.zeros(acc_ref.shape, F32)

    def sel_tile(k0, causal):
        blk_of_key = (k0 + lax.broadcasted_iota(jnp.int32, (tk, HEAD_DIM), 0)) // BLK_SEL
        one_hot = (blk_of_key == lax.broadcasted_iota(jnp.int32, (tk, HEAD_DIM), 1)).astype(BF16)
        keys = jnp.concatenate([ks_ref[pl.ds(k0, tk), :], one_hot], axis=1)
        s = jnp.dot(keys, qt_sel, preferred_element_type=F32)
        if causal:
            kpos = k0 + lax.broadcasted_iota(jnp.int32, (tk, 1), 0)
            s = s + per_head(jnp.where(kpos <= qpos, 0.0, NEG))
        m_old = m_ref[...]
        m_new = jnp.maximum(m_old, jnp.max(s, axis=0, keepdims=True))
        a = jnp.exp2(m_old - m_new)
        p = jnp.exp2(s - m_new)
        l_ref[...] = a * l_ref[...] + jnp.sum(p, axis=0, keepdims=True)
        pv = jnp.dot(vst_ref[:, pl.ds(k0, tk)], p.astype(BF16), preferred_element_type=F32)
        acc_ref[...] = a * acc_ref[...] + pv
        m_ref[...] = m_new

    n_tiles = (q0 + tq + tk - 1) // tk

    def full_tile(kj, carry):
        sel_tile(pl.multiple_of(kj * tk, tk), causal=False)
        return carry

    lax.fori_loop(0, n_tiles - 1, full_tile, 0)
    sel_tile(pl.multiple_of((n_tiles - 1) * tk, tk), causal=True)
    o_sel = acc_ref[...] / l_ref[...]

    band = min(WINDOW + tq, t_len)
    kstart = pl.multiple_of(jnp.clip(q0 - WINDOW, 0, t_len - band), 128)
    dist = qpos - (kstart + lax.broadcasted_iota(jnp.int32, (band, 1), 0))
    bias_w = jnp.where((dist >= 0) & (dist < WINDOW), 0.0, NEG)
    s = jnp.dot(kw_ref[pl.ds(kstart, band), :], qt, preferred_element_type=F32) + per_head(bias_w)
    e = jnp.exp2(s - jnp.max(s, axis=0, keepdims=True))
    pv = jnp.dot(vwt_ref[:, pl.ds(kstart, band)], e.astype(BF16), preferred_element_type=F32)
    o_win = pv / jnp.sum(e, axis=0, keepdims=True)

    gates_t = jax.nn.sigmoid(gate_ref[...]).T
    for g in range(gqa):
        out_t = (gates_t[3 * g:3 * g + 1] * head(o_cmp, g) + gates_t[3 * g + 1:3 * g + 2] * head(o_sel, g)
                 + gates_t[3 * g + 2:3 * g + 3] * head(o_win, g))
        o_ref[:, g * HEAD_DIM:(g + 1) * HEAD_DIM] = out_t.T.astype(o_ref.dtype)


def _attn_prompt(qt, k_cmp, vt_cmp, sel_k, sel_vt, win_k, win_vt, gates, *, bsz, t_len, gqa, tq, tk=512):
    tk = _pick(t_len, tk)
    assert tk % tq == 0 and -(-t_len // BLK_SEL) <= HEAD_DIM
    nt = t_len // tq
    nc = k_cmp.shape[2]
    n_cmp = (t_len - BLK_CMP) // STRIDE_CMP + 1
    n_sel = -(-t_len // BLK_SEL)
    per_head = lambda r, c: pl.BlockSpec((None, None, r, c), lambda b, h, i: (b, h, 0, 0))
    kspec = pl.BlockSpec((t_len, HEAD_DIM), lambda b, h, i: (b, h))
    return pl.pallas_call(
        functools.partial(_attn_prompt_kernel, tq=tq, tk=tk, t_len=t_len, gqa=gqa, n_cmp=n_cmp, n_sel=n_sel),
        grid=(bsz, N_KV, nt),
        in_specs=[pl.BlockSpec((None, None, HEAD_DIM, gqa * tq), lambda b, h, i: (b, h, 0, i)),
                  per_head(nc, HEAD_DIM), per_head(HEAD_DIM, nc),
                  kspec, per_head(HEAD_DIM, t_len), kspec, per_head(HEAD_DIM, t_len),
                  pl.BlockSpec((tq, HEAD_DIM), lambda b, h, i: (b * nt + i, h))],
        out_specs=pl.BlockSpec((tq, gqa * HEAD_DIM), lambda b, h, i: (b * nt + i, h)),
        out_shape=jax.ShapeDtypeStruct((bsz * t_len, N_KV * gqa * HEAD_DIM), BF16),
        scratch_shapes=[pltpu.VMEM((1, gqa * tq), F32), pltpu.VMEM((1, gqa * tq), F32),
                        pltpu.VMEM((HEAD_DIM, gqa * tq), F32)],
        compiler_params=_cparams(("parallel", "parallel", "arbitrary")),
        name="attn_prompt",
    )(qt, k_cmp, vt_cmp, sel_k, sel_vt, win_k, win_vt, gates)


def _attn_sample_a_kernel(q_ref, kc_ref, vct_ref, win_ref, nw_ref, ocmp_ref, owin_ref, idx_ref,
                          *, gqa, n_cmp, n_sel, qpos, wb):
    h = pl.program_id(1)
    qs = q_ref[...]
    nc = kc_ref.shape[0]

    n = lax.broadcasted_iota(jnp.int32, (1, nc), 1)
    ok = (n * STRIDE_CMP + BLK_CMP - 1 <= qpos) & (n < n_cmp)
    s = jnp.where(ok, _dot_nt(qs, kc_ref[...]), NEG)
    e = jnp.where(ok, jnp.exp2(s - jnp.max(s, axis=-1, keepdims=True)), 0.0)
    p_cmp = e / jnp.maximum(jnp.sum(e, axis=-1, keepdims=True), 1e-30)
    ocmp_ref[...] = _dot_nt(p_cmp.astype(BF16), vct_ref[...])

    psum = jnp.sum(p_cmp, axis=0, keepdims=True)
    nsl = 128 * pl.cdiv(n_sel, 128)
    imp = _dot_exact01(jnp.broadcast_to(psum, (8, nc)), _overlap(nc, nsl, n_sel, cmp_axis=0))[0:1]
    score = _select_scores(imp, jnp.full((1, 1), qpos, jnp.int32), n_sel, axis=1)
    cnt = _rank_counts(score, n_sel, axis=1)
    lane = lax.broadcasted_iota(jnp.int32, (1, nsl), 1)
    slot = lax.broadcasted_iota(jnp.int32, (N_SEL, HEAD_DIM), 0)
    idx = jnp.zeros((N_SEL, HEAD_DIM), jnp.int32)
    for k in range(N_SEL):
        blk = jnp.sum(jnp.where(cnt == k, lane, 0), axis=-1, keepdims=True)
        idx = jnp.where(slot == k, blk, idx)
    idx_ref[...] = idx

    width = 2 * N_KV
    rows = win_ref[...].reshape(wb * width, HEAD_DIM).astype(BF16)
    kn = nw_ref[pl.ds(h, 1), :]
    vn = nw_ref[pl.ds(N_KV + h, 1), :]
    col = lax.broadcasted_iota(jnp.int32, (1, wb * width), 1)
    kp = qpos - wb + col // width
    okb = (col % width == h) & (qpos - kp < WINDOW) & (kp >= 0)
    sb = jnp.where(okb, _dot_nt(qs, rows), NEG)
    sn = jnp.sum(qs.astype(F32) * kn, axis=-1, keepdims=True)
    mx = jnp.maximum(jnp.max(sb, axis=-1, keepdims=True), sn)
    pb = jnp.where(okb, jnp.exp2(sb - mx), 0.0)
    pn = jnp.exp2(sn - mx)
    den = jnp.sum(pb, axis=-1, keepdims=True) + pn
    num = jnp.dot(pltpu.roll(pb, N_KV, 1).astype(BF16), rows, preferred_element_type=F32) + pn * vn
    owin_ref[...] = num / den


def _attn_sample_a(q4, k_cmp, vt_cmp, win_all, new_win8, layer, *, gqa, qpos):
    db = q4.shape[0]
    nc = k_cmp.shape[2]
    wb = win_all.shape[1]
    n_cmp = (qpos + 1 - BLK_CMP) // STRIDE_CMP + 1
    n_sel = -(-(qpos + 1) // BLK_SEL)
    head = pl.BlockSpec((None, None, gqa, HEAD_DIM), lambda b, h: (b, h, 0, 0))
    return pl.pallas_call(
        functools.partial(_attn_sample_a_kernel, gqa=gqa, n_cmp=n_cmp, n_sel=n_sel, qpos=qpos, wb=wb),
        grid=(db, N_KV),
        in_specs=[head, pl.BlockSpec((None, None, nc, HEAD_DIM), lambda b, h: (b, h, 0, 0)),
                  pl.BlockSpec((None, None, HEAD_DIM, nc), lambda b, h: (b, h, 0, 0)),
                  pl.BlockSpec((None, wb, 2 * N_KV, HEAD_DIM), lambda b, h: (layer * db + b, 0, 0, 0)),
                  pl.BlockSpec((None, 2 * N_KV, HEAD_DIM), lambda b, h: (b, 0, 0))],
        out_specs=[head, head, pl.BlockSpec((None, None, N_SEL, HEAD_DIM), lambda b, h: (b, h, 0, 0))],
        out_shape=[jax.ShapeDtypeStruct((db, N_KV, gqa, HEAD_DIM), F32),
                   jax.ShapeDtypeStruct((db, N_KV, gqa, HEAD_DIM), F32),
                   jax.ShapeDtypeStruct((db, N_KV, N_SEL, HEAD_DIM), jnp.int32)],
        compiler_params=_cparams(("parallel", "parallel")),
        name="attn_sample_cmp_win",
    )(q4, k_cmp, vt_cmp, win_all, new_win8)


def _attn_sample_b_kernel(idx_ref, phys_ref, *refs, gqa, qpos, n_past_blocks):
    del phys_ref
    q_ref = refs[0]
    blk_refs = refs[1:1 + N_SEL]
    ns_ref, ocmp_ref, owin_ref, gate_ref, o_ref = refs[1 + N_SEL:]
    b, h = pl.program_id(0), pl.program_id(1)
    qs = q_ref[...]
    width = 2 * N_KV
    ncol = BLK_SEL * width
    col = lax.broadcasted_iota(jnp.int32, (1, ncol), 1)

    kn = ns_ref[pl.ds(h, 1), :]
    vn = ns_ref[pl.ds(N_KV + h, 1), :]
    sn = jnp.sum(qs.astype(F32) * kn, axis=-1, keepdims=True)

    scores, mx = [], sn
    for k in range(N_SEL):
        blk = idx_ref[(b * N_KV + h) * N_SEL + k]
        ok = (col % width == h) & (blk * BLK_SEL + col // width <= qpos) & (blk < n_past_blocks)
        s = jnp.where(ok, _dot_nt(qs, blk_refs[k][...].reshape(ncol, HEAD_DIM).astype(BF16)), NEG)
        scores.append(s)
        mx = jnp.maximum(mx, jnp.max(s, axis=-1, keepdims=True))
    pn = jnp.exp2(sn - mx)
    den, num = pn, pn * vn
    for k in range(N_SEL):
        p = jnp.exp2(scores[k] - mx)
        den = den + jnp.sum(p, axis=-1, keepdims=True)
        num = num + jnp.dot(pltpu.roll(p, N_KV, 1).astype(BF16),
                            blk_refs[k][...].reshape(ncol, HEAD_DIM).astype(BF16), preferred_element_type=F32)
    o_sel = num / den

    gates = jax.nn.sigmoid(gate_ref[...])
    o_cmp, o_win = ocmp_ref[...], owin_ref[...]
    rows = []
    for g in range(gqa):
        rows.append(gates[:, 3 * g:3 * g + 1] * o_cmp[g:g + 1] + gates[:, 3 * g + 1:3 * g + 2] * o_sel[g:g + 1]
                    + gates[:, 3 * g + 2:3 * g + 3] * o_win[g:g + 1])
    o_ref[...] = jnp.concatenate(rows, axis=0)


def _attn_sample_b(idx, phys, q4, sel_blocks, new_sel8, o_cmp, o_win, gates4, *, gqa, qpos, n_past_blocks):
    db = q4.shape[0]
    head = pl.BlockSpec((None, None, gqa, HEAD_DIM), lambda b, h, i, p: (b, h, 0, 0))

    def blk_spec(k):
        return pl.BlockSpec((None, BLK_SEL, 2 * N_KV, HEAD_DIM),
                            lambda b, h, i, p: (p[(b * N_KV + h) * N_SEL + k], 0, 0, 0))

    return pl.pallas_call(
        functools.partial(_attn_sample_b_kernel, gqa=gqa, qpos=qpos, n_past_blocks=n_past_blocks),
        grid_spec=pltpu.PrefetchScalarGridSpec(
            num_scalar_prefetch=2,
            grid=(db, N_KV),
            in_specs=[head] + [blk_spec(k) for k in range(N_SEL)]
            + [pl.BlockSpec((None, 2 * N_KV, HEAD_DIM), lambda b, h, i, p: (b, 0, 0)),
               head, head,
               pl.BlockSpec((None, None, 1, HEAD_DIM), lambda b, h, i, p: (b, h, 0, 0))],
            out_specs=head,
        ),
        out_shape=jax.ShapeDtypeStruct((db, N_KV, gqa, HEAD_DIM), F32),
        compiler_params=_cparams(("parallel", "parallel")),
        name="attn_sample_sel",
    )(idx.reshape(-1), phys.reshape(-1), q4, *([sel_blocks] * N_SEL), new_sel8, o_cmp, o_win, gates4)


def _roll_window_kernel(s_ref, n_ref, o_ref, *, wb):
    o_ref[0:wb - 1] = s_ref[1:wb]
    o_ref[wb - 1] = n_ref[...]


def _roll_window(state, new_rows):
    n, wb = state.shape[:2]
    tile = (2 * N_KV, HEAD_DIM)
    return pl.pallas_call(
        functools.partial(_roll_window_kernel, wb=wb),
        grid=(n,),
        in_specs=[pl.BlockSpec((None, wb) + tile, lambda b: (b, 0, 0, 0)), pl.BlockSpec((None,) + tile, lambda b: (b, 0, 0))],
        out_specs=pl.BlockSpec((None, wb) + tile, lambda b: (b, 0, 0, 0)),
        out_shape=jax.ShapeDtypeStruct(state.shape, state.dtype),
        compiler_params=_cparams(("parallel",)),
        name="roll_window",
    )(state, new_rows)


def _pad_rows(a, rows):
    return jnp.pad(a, ((0, rows - a.shape[0]), (0, 0)))


def kernel(x_prompt, x_sample, cache_cmp, cache_sel, state_win, state_conv, page_table, w_in, conv_w, cmp_pe, cmp_w1, cmp_w2, w_out, ln1_g, ln1_b, w_up, w_down, ln2_g, ln2_b):
    bsz, t_len, d = x_prompt.shape
    db, dec_seq, _ = x_sample.shape
    depth = w_in.shape[0]
    cd = state_conv.shape[-1]
    ad = d - cd
    gqa = ad // HEAD_DIM // N_KV
    pool, page_rows = cache_cmp.shape[1], cache_cmp.shape[2]
    n_pages = page_table.shape[1]
    past = n_pages * page_rows
    wb = state_win.shape[2]
    assert dec_seq == 1 and wb == WINDOW and t_len >= WINDOW and 3 * gqa <= HEAD_DIM and db <= SAMPLE_ROWS
    assert page_rows % BLK_SEL == 0 and t_len % page_rows == 0
    alpha = (2.0 * depth) ** 0.25
    nmain = 3 * cd + ad + 6 * KV_DIM
    cpp = page_rows // STRIDE_CMP
    bpp = page_rows // BLK_SEL
    tile = (2 * N_KV, HEAD_DIM)
    tq = _pick(t_len, ATTN_TQ)

    tab_p = _rope_tables(jnp.arange(t_len, dtype=jnp.int32))
    tab_s = _rope_tables(jnp.full((db,), past, jnp.int32))
    pt_prompt = jnp.arange(bsz * (t_len // page_rows), dtype=jnp.int32).reshape(bsz, t_len // page_rows)

    wg = w_in[:, :, nmain:].reshape(depth, d, N_KV, 3 * gqa)
    wg = jnp.pad(wg, ((0, 0), (0, 0), (0, 0), (0, HEAD_DIM - 3 * gqa))).reshape(depth, d, KV_DIM).astype(BF16)
    half = STRIDE_CMP * HEAD_DIM
    w1_flat = cmp_w1.reshape(depth, 2, BLK_CMP * HEAD_DIM, HEAD_DIM).astype(BF16)
    w12 = jnp.concatenate([w1_flat[:, :, :half], w1_flat[:, :, half:]], axis=3)
    pe_flat = jnp.broadcast_to(cmp_pe.reshape(depth, 2, 1, BLK_CMP * HEAD_DIM), (depth, 2, 8, BLK_CMP * HEAD_DIM)).astype(BF16)
    w2 = cmp_w2.astype(BF16)
    w_down_bf = w_down.astype(BF16)
    w_in_nk = jnp.swapaxes(w_in, 1, 2)
    ln = [a.reshape(depth, 1, d) for a in (ln1_g, ln1_b, ln2_g, ln2_b)]

    cmp_pages = cache_cmp.reshape(depth * pool, cpp, STRIDE_CMP, *tile)
    sel_blocks = cache_sel.reshape(depth * pool * bpp, BLK_SEL, *tile)
    win_all = state_win.reshape(depth * db, wb, *tile)

    xp = x_prompt.reshape(bsz * t_len, d)
    xp_bf = xp.astype(BF16)
    xs = _pad_rows(x_sample.reshape(db, d), SAMPLE_ROWS)
    xs_bf = xs.astype(BF16)

    outs = {k: [] for k in ("cmp_p", "sel_p", "win_p", "conv_p", "cmp_s", "sel_s", "win_s", "conv_s")}
    for l in range(depth):
        proj, proj_s = _matmul_ws([xp_bf], [xs_bf], w_in_nk, l, n_out=nmain, out_dtype=F32, w_is_nk=True,
                                  w_single_buffer=True, bn=1024, name="mm_in")
        gates = _matmul(xp_bf, wg, l, out_dtype=F32, bn=KV_DIM, name="mm_gate")
        gates_s = _matmul(xs_bf, wg, l, out_dtype=F32, bn=KV_DIM, name="mm_gate_s")[:db]

        yconv, qt, new_cmp, new_sel, new_win, sel_k, sel_vt, win_k, win_vt, new_conv = _prep_prompt(
            proj, tab_p, conv_w, l, bsz=bsz, t_len=t_len, cd=cd, ad=ad, tt=tq)
        p12 = _compress_chunks(new_cmp.reshape(-1, cpp, STRIDE_CMP, *tile), pt_prompt, w12, l)
        k_cmp, vt_cmp = _compress_finish(p12, pe_flat, w1_flat, w2, l)
        o_attn = _attn_prompt(qt, k_cmp, vt_cmp, sel_k, sel_vt, win_k, win_vt, gates,
                              bsz=bsz, t_len=t_len, gqa=gqa, tq=tq)

        yconv_s, q_s, cmp_row, sel_row, win_row, u_s = _prep_sample(
            proj_s[:db], tab_s, conv_w[l], state_conv[l, :, 0], state_conv[l, :, 1], cd=cd, ad=ad)
        p12_s = _compress_chunks(cmp_pages, page_table + l * pool, w12, l)
        k_cmp_s, vt_cmp_s = _compress_finish(p12_s, pe_flat, w1_flat, w2, l)
        q4 = q_s.reshape(db, N_KV, gqa, HEAD_DIM)
        o_cmp_s, o_win_s, idx = _attn_sample_a(q4, k_cmp_s, vt_cmp_s, win_all, win_row.reshape(db, *tile), l,
                                               gqa=gqa, qpos=past)
        idx = idx[..., 0]
        page = jnp.minimum(idx // bpp, n_pages - 1).reshape(db, N_KV * N_SEL)
        phys = (jnp.take_along_axis(page_table, page, axis=1) + l * pool) * bpp + idx.reshape(db, -1) % bpp
        o_attn_s = _attn_sample_b(idx, phys, q4, sel_blocks, sel_row.reshape(db, *tile), o_cmp_s, o_win_s,
                                  gates_s.reshape(db, N_KV, 1, HEAD_DIM), gqa=gqa, qpos=past, n_past_blocks=n_pages * bpp)
        yconv_s = _pad_rows(yconv_s, SAMPLE_ROWS).astype(BF16)
        o_attn_s = _pad_rows(o_attn_s.reshape(db, ad), SAMPLE_ROWS).astype(BF16)

        h1, h1_s = _matmul_ws([yconv, o_attn], [yconv_s, o_attn_s], w_out, l, n_out=d, out_dtype=F32,
                              res=xp, res_s=xs, alpha=alpha, name="mm_out")
        x1, x1_bf = _layer_norm(h1, ln[0], ln[1], l)
        x1_s, x1_s_bf = _layer_norm(h1_s, ln[0], ln[1], l)
        up, up_s = _matmul_ws([x1_bf], [x1_s_bf], w_up, l, n_out=w_up.shape[2], out_dtype=BF16, act="relu2",
                              w_single_buffer=True, bn=1024, name="mm_up")
        h2 = _matmul(up, w_down_bf, l, out_dtype=F32, res=x1, alpha=alpha, bk=4096, name="mm_down")
        h2_s = _matmul(up_s, w_down_bf, l, out_dtype=F32, res=x1_s, alpha=alpha, bk=2048, name="mm_down_s")
        xp, xp_bf = _layer_norm(h2, ln[2], ln[3], l)
        xs, xs_bf = _layer_norm(h2_s, ln[2], ln[3], l)

        kv_shape = (2, N_KV, HEAD_DIM)
        outs["cmp_p"].append(new_cmp.reshape(bsz, t_len, *kv_shape))
        outs["sel_p"].append(new_sel.reshape(bsz, t_len, *kv_shape))
        outs["win_p"].append(new_win.reshape(bsz, t_len, *kv_shape)[:, t_len - min(WINDOW, t_len):])
        outs["conv_p"].append(new_conv)
        outs["cmp_s"].append(cmp_row.reshape(db, 1, *kv_shape))
        outs["sel_s"].append(sel_row.reshape(db, 1, *kv_shape))
        outs["win_s"].append(win_row.reshape(db, *tile))
        outs["conv_s"].append(jnp.stack([state_conv[l, :, 1], u_s], axis=1))

    st = {k: jnp.stack(v) for k, v in outs.items()}
    new_win_s = _roll_window(win_all, st["win_s"].reshape(depth * db, *tile)).reshape(depth, db, wb, 2, N_KV, HEAD_DIM)
    return (xp.reshape(bsz, t_len, d), xs[:db].reshape(db, 1, d), st["cmp_p"], st["sel_p"], st["win_p"], st["conv_p"],
            st["cmp_s"], st["sel_s"], new_win_s, st["conv_s"])
```

```python
import functools
import math

import jax
import jax.numpy as jnp
from jax import lax
from jax.experimental import pallas as pl
from jax.experimental.pallas import tpu as pltpu

F32 = jnp.float32
BF16 = jnp.bfloat16

HEAD_DIM = 128
N_KV = 4
KV_DIM = N_KV * HEAD_DIM
ROT_DIM = HEAD_DIM // 4
ROPE_THETA = 500000.0
BLK_CMP = 32
STRIDE_CMP = 16
BLK_SEL = 64
N_SEL = 16
WINDOW = 512
CONV_W = 3
LN_EPS = 1e-5
FORCE = 1e4
NEG = -1e30
Q_SCALE = HEAD_DIM ** -0.5 * math.log2(math.e)
SAMPLE_ROWS = 16
ATTN_TQ = 256
CAST_CHUNK = 512

VMEM_LIMIT_BYTES = 60 * 1024 * 1024


def _cparams(sem):
    return pltpu.CompilerParams(dimension_semantics=sem, vmem_limit_bytes=VMEM_LIMIT_BYTES)


def _pick(n, pref):
    if n <= pref:
        return n
    t = pref
    while n % t:
        t //= 2
    return t


def _epilogue(acc, res, *, act, alpha):
    if act == "relu2":
        acc = jnp.square(jnp.maximum(acc, 0.0))
    if res is not None:
        acc = alpha * res + acc
    return acc


def _mm_ws_kernel(*refs, n_x, ksplit, act, alpha, has_res, has_side, w_is_nk):
    w_ref = refs[0]
    x_refs = refs[1:1 + n_x]
    xs_refs = refs[1 + n_x:1 + 2 * n_x]
    pos = 1 + 2 * n_x
    res_ref, ress_ref = (refs[pos], refs[pos + 1]) if has_res else (None, None)
    pos += 2 * has_res
    side_ref = refs[pos] if has_side else None
    pos += has_side
    o_ref, os_ref = refs[pos], refs[pos + 1]
    wbf_ref = refs[-1]

    if has_side:
        refs[pos + 2][...] = side_ref[...].astype(BF16)

    def product(x_list):
        acc = None
        for x_ref, (k0, k1) in zip(x_list, ksplit):
            part = jnp.dot(x_ref[...], wbf_ref[k0:k1, :], preferred_element_type=F32)
            acc = part if acc is None else acc + part
        return acc

    @pl.when(pl.program_id(1) == 0)
    def _():
        kdim = wbf_ref.shape[0]
        ck = math.gcd(kdim, CAST_CHUNK)
        for k0 in range(0, kdim, ck):
            if w_is_nk:
                wbf_ref[k0:k0 + ck, :] = w_ref[:, k0:k0 + ck].T.astype(BF16)
            else:
                wbf_ref[k0:k0 + ck, :] = w_ref[k0:k0 + ck, :].astype(BF16)
        res_s = ress_ref[...] if has_res else None
        os_ref[...] = _epilogue(product(xs_refs), res_s, act=act, alpha=alpha).astype(os_ref.dtype)

    res = res_ref[...] if has_res else None
    o_ref[...] = _epilogue(product(x_refs), res, act=act, alpha=alpha).astype(o_ref.dtype)


def _matmul_ws(xs, xs_s, w, layer, *, n_out, out_dtype, act=None, res=None, res_s=None, alpha=1.0,
               w_is_nk=False, w_single_buffer=False, side=None, bm=1024, bn=512, name="mm"):
    m = xs[0].shape[0]
    ms = xs_s[0].shape[0]
    kdim = w.shape[2] if w_is_nk else w.shape[1]
    bm, bn = _pick(m, bm), _pick(n_out, bn)
    ksplit, k0 = [], 0
    for x in xs:
        ksplit.append((k0, k0 + x.shape[1]))
        k0 += x.shape[1]
    assert k0 == kdim
    w_mode = dict(pipeline_mode=pl.Buffered(1)) if w_single_buffer else {}
    if w_is_nk:
        in_specs = [pl.BlockSpec((None, bn, kdim), lambda j, i: (layer, j, 0), **w_mode)]
    else:
        in_specs = [pl.BlockSpec((None, kdim, bn), lambda j, i: (layer, 0, j), **w_mode)]
    in_specs += [pl.BlockSpec((bm, x.shape[1]), lambda j, i: (i, 0)) for x in xs]
    in_specs += [pl.BlockSpec((ms, x.shape[1]), lambda j, i: (0, 0)) for x in xs_s]
    args = [w, *xs, *xs_s]
    if res is not None:
        in_specs += [pl.BlockSpec((bm, bn), lambda j, i: (i, j)), pl.BlockSpec((ms, bn), lambda j, i: (0, j))]
        args += [res, res_s]
    nj, ni = n_out // bn, m // bm
    out_specs = [pl.BlockSpec((bm, bn), lambda j, i: (i, j)), pl.BlockSpec((ms, bn), lambda j, i: (0, j))]
    out_shape = [jax.ShapeDtypeStruct((m, n_out), out_dtype), jax.ShapeDtypeStruct((ms, n_out), out_dtype)]
    if side is not None:
        rows, cols = side.shape[1] // (nj * ni), side.shape[2]
        assert rows * nj * ni == side.shape[1] and rows % 16 == 0
        in_specs.append(pl.BlockSpec((None, rows, cols), lambda j, i: (layer, j * ni + i, 0)))
        args.append(side)
        out_specs.append(pl.BlockSpec((rows, cols), lambda j, i: (j * ni + i, 0)))
        out_shape.append(jax.ShapeDtypeStruct(side.shape[1:], BF16))
    return pl.pallas_call(
        functools.partial(_mm_ws_kernel, n_x=len(xs), ksplit=tuple(ksplit), act=act, alpha=alpha,
                          has_res=res is not None, has_side=side is not None, w_is_nk=w_is_nk),
        grid=(nj, ni),
        in_specs=in_specs,
        out_specs=out_specs,
        out_shape=out_shape,
        scratch_shapes=[pltpu.VMEM((kdim, bn), BF16)],
        compiler_params=_cparams(("arbitrary", "arbitrary")),
        name=name,
    )(*args)


def _mm_kernel(*refs, nk, act, alpha, has_res):
    x_ref, w_ref = refs[0], refs[1]
    res_ref = refs[2] if has_res else None
    o_ref = refs[2 + has_res]
    acc_ref = (refs[3 + has_res] if len(refs) > 3 + has_res else o_ref) if nk > 1 else None
    part = jnp.dot(x_ref[...], w_ref[...], preferred_element_type=F32)

    def finish(acc):
        res = res_ref[...] if has_res else None
        o_ref[...] = _epilogue(acc, res, act=act, alpha=alpha).astype(o_ref.dtype)

    if nk == 1:
        finish(part)
    else:
        k = pl.program_id(2)

        @pl.when(k == 0)
        def _():
            acc_ref[...] = part

        @pl.when((k > 0) & (k < nk - 1))
        def _():
            acc_ref[...] += part

        @pl.when(k == nk - 1)
        def _():
            finish(acc_ref[...] + part)


def _matmul(x, w, layer=None, *, out_dtype, act=None, res=None, alpha=1.0, bm=1024, bn=1024, bk=4096, name="mm"):
    m, kdim = x.shape
    n = w.shape[-1]
    bm, bn, bk = _pick(m, bm), _pick(n, bn), _pick(kdim, bk)
    nk = kdim // bk
    if layer is None:
        w_spec = pl.BlockSpec((bk, bn), lambda i, j, k: (k, j))
    else:
        w_spec = pl.BlockSpec((None, bk, bn), lambda i, j, k: (layer, k, j))
    in_specs = [pl.BlockSpec((bm, bk), lambda i, j, k: (i, k)), w_spec]
    args = [x, w]
    if res is not None:
        in_specs.append(pl.BlockSpec((bm, bn), lambda i, j, k: (i, j)))
        args.append(res)
    return pl.pallas_call(
        functools.partial(_mm_kernel, nk=nk, act=act, alpha=alpha, has_res=res is not None),
        grid=(m // bm, n // bn, nk),
        in_specs=in_specs,
        out_specs=pl.BlockSpec((bm, bn), lambda i, j, k: (i, j)),
        out_shape=jax.ShapeDtypeStruct((m, n), out_dtype),
        scratch_shapes=[pltpu.VMEM((bm, bn), F32)] if nk > 1 and out_dtype != F32 else [],
        compiler_params=_cparams(("parallel", "parallel", "arbitrary")),
        name=name,
    )(*args)


def _ln_kernel(h_ref, g_ref, b_ref, o_ref, obf_ref):
    h = h_ref[...]
    mu = jnp.mean(h, axis=-1, keepdims=True)
    d = h - mu
    var = jnp.mean(d * d, axis=-1, keepdims=True)
    y = d * lax.rsqrt(var + LN_EPS) * g_ref[...] + b_ref[...]
    o_ref[...] = y
    obf_ref[...] = y.astype(BF16)


def _layer_norm(h, g, b, layer, *, bm=256):
    m, d = h.shape
    bm = _pick(m, bm)
    row = pl.BlockSpec((bm, d), lambda i: (i, 0))
    vec = pl.BlockSpec((None, 1, d), lambda i: (layer, 0, 0))
    return pl.pallas_call(
        _ln_kernel,
        grid=(m // bm,),
        in_specs=[row, vec, vec],
        out_specs=[row, row],
        out_shape=[jax.ShapeDtypeStruct((m, d), F32), jax.ShapeDtypeStruct((m, d), BF16)],
        compiler_params=_cparams(("parallel",)),
        name="layer_norm",
    )(h, g, b)


def _rope_tables(pos):
    half = ROT_DIM // 2
    inv = ROPE_THETA ** (-jnp.arange(half, dtype=F32) / half)
    ang = pos.astype(F32)[:, None] * inv[None, :]
    cos, sin = jnp.cos(ang), jnp.sin(ang)
    n = pos.shape[0]
    one = jnp.ones((n, HEAD_DIM - ROT_DIM), F32)
    zero = jnp.zeros((n, HEAD_DIM - ROT_DIM), F32)
    zh = jnp.zeros((n, half), F32)
    c = jnp.concatenate([cos, cos, one], 1)
    s1 = jnp.concatenate([zh, sin, zero], 1)
    s2 = jnp.concatenate([-sin, zh, zero], 1)
    return c, s1, s2


def _rope(x, c, s1, s2):
    return x * c + pltpu.roll(x, ROT_DIM // 2, 1) * s1 + pltpu.roll(x, HEAD_DIM - ROT_DIM // 2, 1) * s2


def _proj_offsets(cd, ad):
    offs = [0, cd, 2 * cd, 3 * cd, 3 * cd + ad]
    for _ in range(5):
        offs.append(offs[-1] + KV_DIM)
    return offs


def _prep_prompt_kernel(proj_ref, c_ref, s1_ref, s2_ref, cw_ref,
                        yconv_ref, qt_ref, cmp_ref, sel_ref, win_ref, selk_ref, selvt_ref, wink_ref, winvt_ref,
                        nconv_ref, carry_ref, *, cd, ad, tt):
    t = pl.program_id(1)
    c, s1, s2 = c_ref[...], s1_ref[...], s2_ref[...]
    offs = _proj_offsets(cd, ad)

    @pl.when(t == 0)
    def _():
        carry_ref[...] = jnp.zeros_like(carry_ref)

    gb = proj_ref[:, offs[0]:offs[0] + cd]
    u = proj_ref[:, offs[1]:offs[1] + cd] * proj_ref[:, offs[2]:offs[2] + cd]
    row = lax.broadcasted_iota(jnp.int32, (tt, cd), 0)
    c0 = carry_ref[0:1, :]
    c1 = carry_ref[1:2, :]
    u1 = jnp.where(row == 0, c1, pltpu.roll(u, 1, 0))
    u2 = jnp.where(row == 0, c0, jnp.where(row == 1, c1, pltpu.roll(u, 2, 0)))
    y = cw_ref[0:1, :] * u2 + cw_ref[1:2, :] * u1 + cw_ref[2:3, :] * u
    yconv_ref[...] = (gb * y).astype(BF16)
    tail = (proj_ref[tt - 2:tt, offs[1]:offs[1] + cd] * proj_ref[tt - 2:tt, offs[2]:offs[2] + cd])
    carry_ref[...] = tail
    nconv_ref[...] = tail

    gqa = ad // HEAD_DIM // N_KV
    for h in range(ad // HEAD_DIM):
        lo = offs[3] + h * HEAD_DIM
        g = h % gqa
        qt = (_rope(proj_ref[:, lo:lo + HEAD_DIM], c, s1, s2) * Q_SCALE).T
        qt_ref[h // gqa, :, g * tt:(g + 1) * tt] = qt.astype(BF16)

    for k_off, out_ref, k_ref, vt_ref in ((offs[4], cmp_ref, None, None), (offs[6], sel_ref, selk_ref, selvt_ref),
                                          (offs[8], win_ref, wink_ref, winvt_ref)):
        for h in range(N_KV):
            lo = k_off + h * HEAD_DIM
            kr = _rope(proj_ref[:, lo:lo + HEAD_DIM], c, s1, s2)
            v = proj_ref[:, lo + KV_DIM:lo + KV_DIM + HEAD_DIM]
            out_ref[:, h, :] = kr
            out_ref[:, N_KV + h, :] = v
            if k_ref is not None:
                k_ref[:, h * HEAD_DIM:(h + 1) * HEAD_DIM] = kr.astype(BF16)
                vt_ref[h] = v.T.astype(BF16)


def _prep_prompt(proj, tables, conv_w, layer, *, bsz, t_len, cd, ad, tt):
    m, nmain = proj.shape
    nt = t_len // tt
    gqa = ad // HEAD_DIM // N_KV
    rows = lambda w: pl.BlockSpec((tt, w), lambda b, t: (b * nt + t, 0))
    rows3 = pl.BlockSpec((tt, 2 * N_KV, HEAD_DIM), lambda b, t: (b * nt + t, 0, 0))
    tab = pl.BlockSpec((tt, HEAD_DIM), lambda b, t: (t, 0))
    cols = lambda w: pl.BlockSpec((None, N_KV, HEAD_DIM, w), lambda b, t: (b, 0, 0, t))
    kv_rows = jax.ShapeDtypeStruct((m, 2 * N_KV, HEAD_DIM), F32)
    k_rows = jax.ShapeDtypeStruct((m, KV_DIM), BF16)
    v_cols = jax.ShapeDtypeStruct((bsz, N_KV, HEAD_DIM, t_len), BF16)
    return pl.pallas_call(
        functools.partial(_prep_prompt_kernel, cd=cd, ad=ad, tt=tt),
        grid=(bsz, nt),
        in_specs=[rows(nmain), tab, tab, tab, pl.BlockSpec((None, CONV_W, cd), lambda b, t: (layer, 0, 0))],
        out_specs=[rows(cd), cols(gqa * tt), rows3, rows3, rows3, rows(KV_DIM), cols(tt), rows(KV_DIM), cols(tt),
                   pl.BlockSpec((None, CONV_W - 1, cd), lambda b, t: (b, 0, 0))],
        out_shape=[jax.ShapeDtypeStruct((m, cd), BF16), jax.ShapeDtypeStruct((bsz, N_KV, HEAD_DIM, gqa * t_len), BF16),
                   kv_rows, kv_rows, kv_rows, k_rows, v_cols, k_rows, v_cols,
                   jax.ShapeDtypeStruct((bsz, CONV_W - 1, cd), F32)],
        scratch_shapes=[pltpu.VMEM((CONV_W - 1, cd), F32)],
        compiler_params=_cparams(("parallel", "arbitrary")),
        name="prep_prompt",
    )(proj, *tables, conv_w)


def _prep_sample_kernel(proj_ref, c_ref, s1_ref, s2_ref, cw_ref, b0_ref, b1_ref,
                        yconv_ref, q_ref, cmp_ref, sel_ref, win_ref, u_ref, *, cd, ad):
    c, s1, s2 = c_ref[...], s1_ref[...], s2_ref[...]
    offs = _proj_offsets(cd, ad)
    gb = proj_ref[:, offs[0]:offs[0] + cd]
    u = proj_ref[:, offs[1]:offs[1] + cd] * proj_ref[:, offs[2]:offs[2] + cd]
    y = cw_ref[0:1, :] * b0_ref[...] + cw_ref[1:2, :] * b1_ref[...] + cw_ref[2:3, :] * u
    yconv_ref[...] = gb * y
    u_ref[...] = u
    for h in range(ad // HEAD_DIM):
        lo = offs[3] + h * HEAD_DIM
        q_ref[:, h * HEAD_DIM:(h + 1) * HEAD_DIM] = (_rope(proj_ref[:, lo:lo + HEAD_DIM], c, s1, s2) * Q_SCALE).astype(BF16)
    for k_off, out_ref in ((offs[4], cmp_ref), (offs[6], sel_ref), (offs[8], win_ref)):
        for h in range(N_KV):
            lo = k_off + h * HEAD_DIM
            out_ref[:, h * HEAD_DIM:(h + 1) * HEAD_DIM] = _rope(proj_ref[:, lo:lo + HEAD_DIM], c, s1, s2)
            out_ref[:, KV_DIM + h * HEAD_DIM:KV_DIM + (h + 1) * HEAD_DIM] = proj_ref[:, lo + KV_DIM:lo + KV_DIM + HEAD_DIM]


def _prep_sample(proj, tables, conv_w, buf0, buf1, *, cd, ad):
    db = proj.shape[0]
    kv2 = 2 * KV_DIM
    full = lambda a: pl.BlockSpec(a.shape, lambda i: (0,) * a.ndim)
    ins = [proj, *tables, conv_w, buf0, buf1]
    outs = [jax.ShapeDtypeStruct((db, cd), F32), jax.ShapeDtypeStruct((db, ad), BF16),
            jax.ShapeDtypeStruct((db, kv2), F32), jax.ShapeDtypeStruct((db, kv2), F32),
            jax.ShapeDtypeStruct((db, kv2), F32), jax.ShapeDtypeStruct((db, cd), F32)]
    return pl.pallas_call(
        functools.partial(_prep_sample_kernel, cd=cd, ad=ad),
        grid=(1,),
        in_specs=[full(a) for a in ins],
        out_specs=[full(o) for o in outs],
        out_shape=outs,
        compiler_params=_cparams(("arbitrary",)),
        name="prep_sample",
    )(*ins)


def _cmp12_kernel(pt_ref, *refs, npg):
    del pt_ref
    pages, w_ref, o_ref = refs[:npg], refs[npg], refs[npg + 1]
    cpp = pages[0].shape[0]
    lhs = ([], [])
    for l in range(STRIDE_CMP):
        per_page = [pltpu.einshape("cjd->jcd", pr[:, l, :, :]) for pr in pages]
        for j in range(2):
            lhs[j].append(jnp.concatenate([xt[j * N_KV + h] for h in range(N_KV) for xt in per_page], axis=0))
    for j in range(2):
        rows = jnp.concatenate(lhs[j], axis=1).astype(BF16)
        out = jnp.dot(rows, w_ref[j], preferred_element_type=F32)
        o_ref[j] = out.reshape(N_KV, npg * cpp, 2 * HEAD_DIM)


def _compress_chunks(pages5, page_ids, w12, layer, *, npg=16):
    bsz, n_pages = page_ids.shape
    cpp = pages5.shape[1]
    assert cpp == 8 and pages5.shape[2] == STRIDE_CMP, "one (8, 128) tile holds a row of all chunks of a page"
    npg = _pick(n_pages, npg)

    def page_spec(i):
        return pl.BlockSpec((None, cpp, STRIDE_CMP, 2 * N_KV, HEAD_DIM),
                            lambda b, g, pt: (pt[b * n_pages + g * npg + i], 0, 0, 0, 0))

    return pl.pallas_call(
        functools.partial(_cmp12_kernel, npg=npg),
        grid_spec=pltpu.PrefetchScalarGridSpec(
            num_scalar_prefetch=1,
            grid=(bsz, n_pages // npg),
            in_specs=[page_spec(i) for i in range(npg)]
            + [pl.BlockSpec((None,) + w12.shape[1:], lambda b, g, pt: (layer, 0, 0, 0))],
            out_specs=pl.BlockSpec((None, 2, N_KV, npg * cpp, 2 * HEAD_DIM), lambda b, g, pt: (b, 0, 0, g, 0)),
        ),
        out_shape=jax.ShapeDtypeStruct((bsz, 2, N_KV, n_pages * cpp, 2 * HEAD_DIM), F32),
        compiler_params=_cparams(("parallel", "parallel")),
        name="compress_chunks",
    )(page_ids.reshape(-1), *([pages5] * npg), w12)


def _cmp_finish_kernel(p_ref, pe_ref, w1_ref, w2_ref, w2t_ref, k_ref, vt_ref, *, nc):
    for j in range(2):
        bias = jnp.dot(pe_ref[j], w1_ref[j], preferred_element_type=F32)[0:1]
        for h in range(N_KV):
            p = p_ref[j, h]
            pre = p[:, :HEAD_DIM] + pltpu.roll(p[:, HEAD_DIM:], nc - 1, 0) + bias
            act = jax.nn.gelu(pre).astype(BF16)
            if j == 0:
                k_ref[h] = jnp.dot(act, w2_ref[j], preferred_element_type=F32).astype(BF16)
            else:
                vt_ref[h] = _dot_nt(w2t_ref[j], act).astype(BF16)


def _compress_finish(p12, pe_flat, w1_flat, w2, layer):
    bsz, _, _, nc, _ = p12.shape
    per_layer = lambda a: pl.BlockSpec((None,) + a.shape[1:], lambda b: (layer,) + (0,) * (a.ndim - 1))
    return pl.pallas_call(
        functools.partial(_cmp_finish_kernel, nc=nc),
        grid=(bsz,),
        in_specs=[pl.BlockSpec((None, 2, N_KV, nc, 2 * HEAD_DIM), lambda b: (b, 0, 0, 0, 0)),
                  per_layer(pe_flat), per_layer(w1_flat), per_layer(w2), per_layer(w2)],
        out_specs=[pl.BlockSpec((None, N_KV, nc, HEAD_DIM), lambda b: (b, 0, 0, 0)),
                   pl.BlockSpec((None, N_KV, HEAD_DIM, nc), lambda b: (b, 0, 0, 0))],
        out_shape=[jax.ShapeDtypeStruct((bsz, N_KV, nc, HEAD_DIM), BF16),
                   jax.ShapeDtypeStruct((bsz, N_KV, HEAD_DIM, nc), BF16)],
        compiler_params=_cparams(("parallel",)),
        name="compress_finish",
    )(p12, pe_flat, w1_flat, w2, jnp.swapaxes(w2, 2, 3))


def _dot_nt(a, b):
    return lax.dot_general(a, b, (((1,), (1,)), ((), ())), preferred_element_type=F32)


def _split3(p):
    hi = p.astype(BF16)
    r1 = p - hi.astype(F32)
    mid = r1.astype(BF16)
    return hi, mid, (r1 - mid.astype(F32)).astype(BF16)


def _dot_exact01(p, onehot_bf16):
    return sum(jnp.dot(t, onehot_bf16, preferred_element_type=F32) for t in _split3(p))


def _overlap(n_cmp_pad, n_sel_pad, n_sel, *, cmp_axis):
    shape = (n_cmp_pad, n_sel_pad) if cmp_axis == 0 else (n_sel_pad, n_cmp_pad)
    n = lax.broadcasted_iota(jnp.int32, shape, cmp_axis) * STRIDE_CMP
    j = lax.broadcasted_iota(jnp.int32, shape, 1 - cmp_axis)
    ov = (n < j * BLK_SEL + BLK_SEL) & (n + BLK_CMP > j * BLK_SEL) & (j < n_sel)
    return ov.astype(BF16)


def _select_scores(imp, qpos, n_sel, *, axis):
    j = lax.broadcasted_iota(jnp.int32, imp.shape, axis)
    jq = qpos // BLK_SEL
    valid = j * BLK_SEL <= qpos
    forced = (j == 0) | (j == jq) | (j == jq - 1)
    score = jnp.where(valid, jnp.where(forced, FORCE, imp), -FORCE)
    return jnp.where(j < n_sel, score, -jnp.inf)


def _rank_counts(score, n_sel, *, axis):
    j = lax.broadcasted_iota(jnp.int32, score.shape, axis)
    cnt = jnp.zeros(score.shape, jnp.int32)
    for i in range(n_sel):
        si = score[i:i + 1, :] if axis == 0 else score[:, i:i + 1]
        beats = (si > score) | ((si == score) & (i < j))
        cnt = cnt + beats.astype(jnp.int32)
    return cnt


def _attn_prompt_kernel(qt_ref, kc_ref, vct_ref, ks_ref, vst_ref, kw_ref, vwt_ref, gate_ref, o_ref,
                        m_ref, l_ref, acc_ref, *, tq, tk, t_len, gqa, n_cmp, n_sel):
    i = pl.program_id(2)
    q0 = i * tq
    qpos = q0 + lax.broadcasted_iota(jnp.int32, (1, tq), 1)
    nc = kc_ref.shape[0]
    nsr = 8 * pl.cdiv(n_sel, 8)
    qt = qt_ref[...]

    def per_head(x):
        return jnp.concatenate([x] * gqa, axis=1)

    def head(x, g):
        return x[:, g * tq:(g + 1) * tq]

    def with_ones(vals_t):
        return jnp.concatenate([vals_t, jnp.ones((16, vals_t.shape[1]), BF16)], axis=0)

    n = lax.broadcasted_iota(jnp.int32, (nc, 1), 0)
    bias_c = jnp.where((n * STRIDE_CMP + BLK_CMP - 1 <= qpos) & (n < n_cmp), 0.0, NEG)
    has_block = (qpos >= BLK_CMP - 1).astype(F32)
    s = jnp.dot(kc_ref[...], qt, preferred_element_type=F32) + per_head(bias_c)
    e = jnp.exp2(s - jnp.max(s, axis=0, keepdims=True))
    p_cmp = e * (per_head(has_block) / jnp.sum(e, axis=0, keepdims=True))
    o_cmp = jnp.dot(vct_ref[...], p_cmp.astype(BF16), preferred_element_type=F32)

    def all_started():
        j = lax.broadcasted_iota(jnp.int32, (nsr, tq), 0)
        return jnp.where(j * BLK_SEL <= qpos, 0.0, NEG)

    def top_k():
        p_sum = sum(head(p_cmp, g) for g in range(gqa))
        ov = _overlap(nc, nsr, n_sel, cmp_axis=1)
        imp = sum(jnp.dot(ov, t, preferred_element_type=F32) for t in _split3(p_sum))
        score = _select_scores(imp, qpos, n_sel, axis=0)
        return jnp.where(_rank_counts(score, n_sel, axis=0) < N_SEL, 0.0, NEG)

    sel_bias = lax.cond((q0 + tq - 1) // BLK_SEL + 1 <= N_SEL, all_started, top_k).astype(BF16)

    sel_bias = jnp.concatenate([sel_bias, jnp.zeros((HEAD_DIM - nsr, tq), BF16)], axis=0)
    qt_sel = jnp.concatenate([qt, per_head(sel_bias)], axis=0)

    m_ref[...] = jnp.full(m_ref.shape, NEG, F32)
    l_ref[...] = jnp.zeros(l_ref.shape, F32)
    acc_ref[...] = jnp.zeros(acc_ref.shape, F32)

    def sel_tile(k0, causal):
        blk_of_key = (k0 + lax.broadcasted_iota(jnp.int32, (tk, HEAD_DIM), 0)) // BLK_SEL
        one_hot = (blk_of_key == lax.broadcasted_iota(jnp.int32, (tk, HEAD_DIM), 1)).astype(BF16)
        keys = jnp.concatenate([ks_ref[pl.ds(k0, tk), :], one_hot], axis=1)
        s = jnp.dot(keys, qt_sel, preferred_element_type=F32)
        if causal:
            kpos = k0 + lax.broadcasted_iota(jnp.int32, (tk, 1), 0)
            s = s + per_head(jnp.where(kpos <= qpos, 0.0, NEG))
        m_old = m_ref[...]
        m_new = jnp.maximum(m_old, jnp.max(s, axis=0, keepdims=True))
        a = jnp.exp2(m_old - m_new)
        p = jnp.exp2((s - m_new).astype(BF16))
        pv = jnp.dot(with_ones(vst_ref[:, pl.ds(k0, tk)]), p, preferred_element_type=F32)
        l_ref[...] = a * l_ref[...] + pv[HEAD_DIM:HEAD_DIM + 1]
        acc_ref[...] = a * acc_ref[...] + pv[:HEAD_DIM]
        m_ref[...] = m_new

    n_tiles = (q0 + tq + tk - 1) // tk

    def full_tile(kj, carry):
        sel_tile(pl.multiple_of(kj * tk, tk), causal=False)
        return carry

    lax.fori_loop(0, n_tiles - 1, full_tile, 0)
    sel_tile(pl.multiple_of((n_tiles - 1) * tk, tk), causal=True)
    o_sel = acc_ref[...] / l_ref[...]

    band = min(WINDOW + tq, t_len)
    kstart = pl.multiple_of(jnp.clip(q0 - WINDOW, 0, t_len - band), 128)
    dist = qpos - (kstart + lax.broadcasted_iota(jnp.int32, (band, 1), 0))
    bias_w = jnp.where((dist >= 0) & (dist < WINDOW), 0.0, NEG)
    s = jnp.dot(kw_ref[pl.ds(kstart, band), :], qt, preferred_element_type=F32) + per_head(bias_w)
    e = jnp.exp2((s - jnp.max(s, axis=0, keepdims=True)).astype(BF16))
    pv = jnp.dot(with_ones(vwt_ref[:, pl.ds(kstart, band)]), e, preferred_element_type=F32)
    o_win = pv[:HEAD_DIM] / pv[HEAD_DIM:HEAD_DIM + 1]

    gates_t = jax.nn.sigmoid(gate_ref[...]).T
    for g in range(gqa):
        out_t = (gates_t[3 * g:3 * g + 1] * head(o_cmp, g) + gates_t[3 * g + 1:3 * g + 2] * head(o_sel, g)
                 + gates_t[3 * g + 2:3 * g + 3] * head(o_win, g))
        o_ref[:, g * HEAD_DIM:(g + 1) * HEAD_DIM] = out_t.T.astype(o_ref.dtype)


def _attn_prompt(qt, k_cmp, vt_cmp, sel_k, sel_vt, win_k, win_vt, gates, *, bsz, t_len, gqa, tq, tk=512):
    tk = _pick(t_len, tk)
    assert tk % tq == 0 and -(-t_len // BLK_SEL) <= HEAD_DIM
    nt = t_len // tq
    nc = k_cmp.shape[2]
    n_cmp = (t_len - BLK_CMP) // STRIDE_CMP + 1
    n_sel = -(-t_len // BLK_SEL)
    per_head = lambda r, c: pl.BlockSpec((None, None, r, c), lambda b, h, i: (b, h, 0, 0))
    kspec = pl.BlockSpec((t_len, HEAD_DIM), lambda b, h, i: (b, h))
    return pl.pallas_call(
        functools.partial(_attn_prompt_kernel, tq=tq, tk=tk, t_len=t_len, gqa=gqa, n_cmp=n_cmp, n_sel=n_sel),
        grid=(bsz, N_KV, nt),
        in_specs=[pl.BlockSpec((None, None, HEAD_DIM, gqa * tq), lambda b, h, i: (b, h, 0, i)),
                  per_head(nc, HEAD_DIM), per_head(HEAD_DIM, nc),
                  kspec, per_head(HEAD_DIM, t_len), kspec, per_head(HEAD_DIM, t_len),
                  pl.BlockSpec((tq, HEAD_DIM), lambda b, h, i: (b * nt + i, h))],
        out_specs=pl.BlockSpec((tq, gqa * HEAD_DIM), lambda b, h, i: (b * nt + i, h)),
        out_shape=jax.ShapeDtypeStruct((bsz * t_len, N_KV * gqa * HEAD_DIM), BF16),
        scratch_shapes=[pltpu.VMEM((1, gqa * tq), F32), pltpu.VMEM((1, gqa * tq), F32),
                        pltpu.VMEM((HEAD_DIM, gqa * tq), F32)],
        compiler_params=_cparams(("parallel", "parallel", "arbitrary")),
        name="attn_prompt",
    )(qt, k_cmp, vt_cmp, sel_k, sel_vt, win_k, win_vt, gates)


def _attn_sample_a_kernel(q_ref, kc_ref, vct_ref, win_ref, nw_ref, ocmp_ref, owin_ref, idx_ref,
                          *, gqa, n_cmp, n_sel, qpos, wb):
    h = pl.program_id(1)
    qs = q_ref[...]
    nc = kc_ref.shape[0]

    n = lax.broadcasted_iota(jnp.int32, (1, nc), 1)
    ok = (n * STRIDE_CMP + BLK_CMP - 1 <= qpos) & (n < n_cmp)
    s = jnp.where(ok, _dot_nt(qs, kc_ref[...]), NEG)
    e = jnp.where(ok, jnp.exp2(s - jnp.max(s, axis=-1, keepdims=True)), 0.0)
    p_cmp = e / jnp.maximum(jnp.sum(e, axis=-1, keepdims=True), 1e-30)
    ocmp_ref[...] = _dot_nt(p_cmp.astype(BF16), vct_ref[...])

    psum = jnp.sum(p_cmp, axis=0, keepdims=True)
    nsl = 128 * pl.cdiv(n_sel, 128)
    imp = _dot_exact01(jnp.broadcast_to(psum, (8, nc)), _overlap(nc, nsl, n_sel, cmp_axis=0))[0:1]
    score = _select_scores(imp, jnp.full((1, 1), qpos, jnp.int32), n_sel, axis=1)
    cnt = _rank_counts(score, n_sel, axis=1)
    lane = lax.broadcasted_iota(jnp.int32, (1, nsl), 1)
    slot = lax.broadcasted_iota(jnp.int32, (N_SEL, HEAD_DIM), 0)
    idx = jnp.zeros((N_SEL, HEAD_DIM), jnp.int32)
    for k in range(N_SEL):
        blk = jnp.sum(jnp.where(cnt == k, lane, 0), axis=-1, keepdims=True)
        idx = jnp.where(slot == k, blk, idx)
    idx_ref[...] = idx

    width = 2 * N_KV
    rows = win_ref[...].reshape(wb * width, HEAD_DIM).astype(BF16)
    kn = nw_ref[pl.ds(h, 1), :]
    vn = nw_ref[pl.ds(N_KV + h, 1), :]
    col = lax.broadcasted_iota(jnp.int32, (1, wb * width), 1)
    kp = qpos - wb + col // width
    okb = (col % width == h) & (qpos - kp < WINDOW) & (kp >= 0)
    sb = jnp.where(okb, _dot_nt(qs, rows), NEG)
    sn = jnp.sum(qs.astype(F32) * kn, axis=-1, keepdims=True)
    mx = jnp.maximum(jnp.max(sb, axis=-1, keepdims=True), sn)
    pb = jnp.where(okb, jnp.exp2(sb - mx), 0.0)
    pn = jnp.exp2(sn - mx)
    den = jnp.sum(pb, axis=-1, keepdims=True) + pn
    num = jnp.dot(pltpu.roll(pb, N_KV, 1).astype(BF16), rows, preferred_element_type=F32) + pn * vn
    owin_ref[...] = num / den


def _attn_sample_a(q4, k_cmp, vt_cmp, win_all, new_win8, layer, *, gqa, qpos):
    db = q4.shape[0]
    nc = k_cmp.shape[2]
    wb = win_all.shape[1]
    n_cmp = (qpos + 1 - BLK_CMP) // STRIDE_CMP + 1
    n_sel = -(-(qpos + 1) // BLK_SEL)
    head = pl.BlockSpec((None, None, gqa, HEAD_DIM), lambda b, h: (b, h, 0, 0))
    return pl.pallas_call(
        functools.partial(_attn_sample_a_kernel, gqa=gqa, n_cmp=n_cmp, n_sel=n_sel, qpos=qpos, wb=wb),
        grid=(db, N_KV),
        in_specs=[head, pl.BlockSpec((None, None, nc, HEAD_DIM), lambda b, h: (b, h, 0, 0)),
                  pl.BlockSpec((None, None, HEAD_DIM, nc), lambda b, h: (b, h, 0, 0)),
                  pl.BlockSpec((None, wb, 2 * N_KV, HEAD_DIM), lambda b, h: (layer * db + b, 0, 0, 0)),
                  pl.BlockSpec((None, 2 * N_KV, HEAD_DIM), lambda b, h: (b, 0, 0))],
        out_specs=[head, head, pl.BlockSpec((None, None, N_SEL, HEAD_DIM), lambda b, h: (b, h, 0, 0))],
        out_shape=[jax.ShapeDtypeStruct((db, N_KV, gqa, HEAD_DIM), F32),
                   jax.ShapeDtypeStruct((db, N_KV, gqa, HEAD_DIM), F32),
                   jax.ShapeDtypeStruct((db, N_KV, N_SEL, HEAD_DIM), jnp.int32)],
        compiler_params=_cparams(("parallel", "parallel")),
        name="attn_sample_cmp_win",
    )(q4, k_cmp, vt_cmp, win_all, new_win8)


def _attn_sample_b_kernel(idx_ref, phys_ref, *refs, gqa, qpos, n_past_blocks):
    del phys_ref
    q_ref = refs[0]
    blk_refs = refs[1:1 + N_SEL]
    ns_ref, ocmp_ref, owin_ref, gate_ref, o_ref = refs[1 + N_SEL:]
    b, h = pl.program_id(0), pl.program_id(1)
    qs = q_ref[...]
    width = 2 * N_KV
    ncol = BLK_SEL * width
    col = lax.broadcasted_iota(jnp.int32, (1, ncol), 1)

    kn = ns_ref[pl.ds(h, 1), :]
    vn = ns_ref[pl.ds(N_KV + h, 1), :]
    sn = jnp.sum(qs.astype(F32) * kn, axis=-1, keepdims=True)

    scores, mx = [], sn
    for k in range(N_SEL):
        blk = idx_ref[(b * N_KV + h) * N_SEL + k]
        ok = (col % width == h) & (blk * BLK_SEL + col // width <= qpos) & (blk < n_past_blocks)
        s = jnp.where(ok, _dot_nt(qs, blk_refs[k][...].reshape(ncol, HEAD_DIM).astype(BF16)), NEG)
        scores.append(s)
        mx = jnp.maximum(mx, jnp.max(s, axis=-1, keepdims=True))
    pn = jnp.exp2(sn - mx)
    den, num = pn, pn * vn
    for k in range(N_SEL):
        p = jnp.exp2(scores[k] - mx)
        den = den + jnp.sum(p, axis=-1, keepdims=True)
        num = num + jnp.dot(pltpu.roll(p, N_KV, 1).astype(BF16),
                            blk_refs[k][...].reshape(ncol, HEAD_DIM).astype(BF16), preferred_element_type=F32)
    o_sel = num / den

    gates = jax.nn.sigmoid(gate_ref[...])
    o_cmp, o_win = ocmp_ref[...], owin_ref[...]
    rows = []
    for g in range(gqa):
        rows.append(gates[:, 3 * g:3 * g + 1] * o_cmp[g:g + 1] + gates[:, 3 * g + 1:3 * g + 2] * o_sel[g:g + 1]
                    + gates[:, 3 * g + 2:3 * g + 3] * o_win[g:g + 1])
    o_ref[...] = jnp.concatenate(rows, axis=0)


def _attn_sample_b(idx, phys, q4, sel_blocks, new_sel8, o_cmp, o_win, gates4, *, gqa, qpos, n_past_blocks):
    db = q4.shape[0]
    head = pl.BlockSpec((None, None, gqa, HEAD_DIM), lambda b, h, i, p: (b, h, 0, 0))

    def blk_spec(k):
        return pl.BlockSpec((None, BLK_SEL, 2 * N_KV, HEAD_DIM),
                            lambda b, h, i, p: (p[(b * N_KV + h) * N_SEL + k], 0, 0, 0))

    return pl.pallas_call(
        functools.partial(_attn_sample_b_kernel, gqa=gqa, qpos=qpos, n_past_blocks=n_past_blocks),
        grid_spec=pltpu.PrefetchScalarGridSpec(
            num_scalar_prefetch=2,
            grid=(db, N_KV),
            in_specs=[head] + [blk_spec(k) for k in range(N_SEL)]
            + [pl.BlockSpec((None, 2 * N_KV, HEAD_DIM), lambda b, h, i, p: (b, 0, 0)),
               head, head,
               pl.BlockSpec((None, None, 1, HEAD_DIM), lambda b, h, i, p: (b, h, 0, 0))],
            out_specs=head,
        ),
        out_shape=jax.ShapeDtypeStruct((db, N_KV, gqa, HEAD_DIM), F32),
        compiler_params=_cparams(("parallel", "parallel")),
        name="attn_sample_sel",
    )(idx.reshape(-1), phys.reshape(-1), q4, *([sel_blocks] * N_SEL), new_sel8, o_cmp, o_win, gates4)


def _roll_window_kernel(s_ref, n_ref, o_ref, *, wb):
    o_ref[0:wb - 1] = s_ref[1:wb]
    o_ref[wb - 1] = n_ref[...]


def _roll_window(state, new_rows):
    n, wb = state.shape[:2]
    tile = (2 * N_KV, HEAD_DIM)
    return pl.pallas_call(
        functools.partial(_roll_window_kernel, wb=wb),
        grid=(n,),
        in_specs=[pl.BlockSpec((None, wb) + tile, lambda b: (b, 0, 0, 0)), pl.BlockSpec((None,) + tile, lambda b: (b, 0, 0))],
        out_specs=pl.BlockSpec((None, wb) + tile, lambda b: (b, 0, 0, 0)),
        out_shape=jax.ShapeDtypeStruct(state.shape, state.dtype),
        compiler_params=_cparams(("parallel",)),
        name="roll_window",
    )(state, new_rows)


def _pad_rows(a, rows):
    return jnp.pad(a, ((0, rows - a.shape[0]), (0, 0)))


def kernel(x_prompt, x_sample, cache_cmp, cache_sel, state_win, state_conv, page_table, w_in, conv_w, cmp_pe, cmp_w1, cmp_w2, w_out, ln1_g, ln1_b, w_up, w_down, ln2_g, ln2_b):
    bsz, t_len, d = x_prompt.shape
    db, dec_seq, _ = x_sample.shape
    depth = w_in.shape[0]
    cd = state_conv.shape[-1]
    ad = d - cd
    gqa = ad // HEAD_DIM // N_KV
    pool, page_rows = cache_cmp.shape[1], cache_cmp.shape[2]
    n_pages = page_table.shape[1]
    past = n_pages * page_rows
    wb = state_win.shape[2]
    assert dec_seq == 1 and wb == WINDOW and t_len >= WINDOW and 3 * gqa <= HEAD_DIM and db <= SAMPLE_ROWS
    assert page_rows % BLK_SEL == 0 and t_len % page_rows == 0
    alpha = (2.0 * depth) ** 0.25
    nmain = 3 * cd + ad + 6 * KV_DIM
    cpp = page_rows // STRIDE_CMP
    bpp = page_rows // BLK_SEL
    tile = (2 * N_KV, HEAD_DIM)
    tq = _pick(t_len, ATTN_TQ)

    tab_p = _rope_tables(jnp.arange(t_len, dtype=jnp.int32))
    tab_s = _rope_tables(jnp.full((db,), past, jnp.int32))
    pt_prompt = jnp.arange(bsz * (t_len // page_rows), dtype=jnp.int32).reshape(bsz, t_len // page_rows)

    wg = w_in[:, :, nmain:].reshape(depth, d, N_KV, 3 * gqa)
    wg = jnp.pad(wg, ((0, 0), (0, 0), (0, 0), (0, HEAD_DIM - 3 * gqa))).reshape(depth, d, KV_DIM).astype(BF16)
    half = STRIDE_CMP * HEAD_DIM
    w1_flat = cmp_w1.reshape(depth, 2, BLK_CMP * HEAD_DIM, HEAD_DIM).astype(BF16)
    w12 = jnp.concatenate([w1_flat[:, :, :half], w1_flat[:, :, half:]], axis=3)
    pe_flat = jnp.broadcast_to(cmp_pe.reshape(depth, 2, 1, BLK_CMP * HEAD_DIM), (depth, 2, 8, BLK_CMP * HEAD_DIM)).astype(BF16)
    w2 = cmp_w2.astype(BF16)
    w_in_nk = jnp.swapaxes(w_in, 1, 2)
    ln = [a.reshape(depth, 1, d) for a in (ln1_g, ln1_b, ln2_g, ln2_b)]

    cmp_pages = cache_cmp.reshape(depth * pool, cpp, STRIDE_CMP, *tile)
    sel_blocks = cache_sel.reshape(depth * pool * bpp, BLK_SEL, *tile)
    win_all = state_win.reshape(depth * db, wb, *tile)

    xp = x_prompt.reshape(bsz * t_len, d)
    xp_bf = xp.astype(BF16)
    xs = _pad_rows(x_sample.reshape(db, d), SAMPLE_ROWS)
    xs_bf = xs.astype(BF16)

    outs = {k: [] for k in ("cmp_p", "sel_p", "win_p", "conv_p", "cmp_s", "sel_s", "win_s", "conv_s")}
    for l in range(depth):
        proj, proj_s = _matmul_ws([xp_bf], [xs_bf], w_in_nk, l, n_out=nmain, out_dtype=F32, w_is_nk=True,
                                  w_single_buffer=True, bn=1024, name="mm_in")
        gates = _matmul(xp_bf, wg, l, out_dtype=F32, bn=KV_DIM, name="mm_gate")
        gates_s = _matmul(xs_bf, wg, l, out_dtype=F32, bn=KV_DIM, name="mm_gate_s")[:db]

        yconv, qt, new_cmp, new_sel, new_win, sel_k, sel_vt, win_k, win_vt, new_conv = _prep_prompt(
            proj, tab_p, conv_w, l, bsz=bsz, t_len=t_len, cd=cd, ad=ad, tt=tq)
        p12 = _compress_chunks(new_cmp.reshape(-1, cpp, STRIDE_CMP, *tile), pt_prompt, w12, l)
        k_cmp, vt_cmp = _compress_finish(p12, pe_flat, w1_flat, w2, l)
        o_attn = _attn_prompt(qt, k_cmp, vt_cmp, sel_k, sel_vt, win_k, win_vt, gates,
                              bsz=bsz, t_len=t_len, gqa=gqa, tq=tq)

        yconv_s, q_s, cmp_row, sel_row, win_row, u_s = _prep_sample(
            proj_s[:db], tab_s, conv_w[l], state_conv[l, :, 0], state_conv[l, :, 1], cd=cd, ad=ad)
        p12_s = _compress_chunks(cmp_pages, page_table + l * pool, w12, l)
        k_cmp_s, vt_cmp_s = _compress_finish(p12_s, pe_flat, w1_flat, w2, l)
        q4 = q_s.reshape(db, N_KV, gqa, HEAD_DIM)
        o_cmp_s, o_win_s, idx = _attn_sample_a(q4, k_cmp_s, vt_cmp_s, win_all, win_row.reshape(db, *tile), l,
                                               gqa=gqa, qpos=past)
        idx = idx[..., 0]
        page = jnp.minimum(idx // bpp, n_pages - 1).reshape(db, N_KV * N_SEL)
        phys = (jnp.take_along_axis(page_table, page, axis=1) + l * pool) * bpp + idx.reshape(db, -1) % bpp
        o_attn_s = _attn_sample_b(idx, phys, q4, sel_blocks, sel_row.reshape(db, *tile), o_cmp_s, o_win_s,
                                  gates_s.reshape(db, N_KV, 1, HEAD_DIM), gqa=gqa, qpos=past, n_past_blocks=n_pages * bpp)
        yconv_s = _pad_rows(yconv_s, SAMPLE_ROWS).astype(BF16)
        o_attn_s = _pad_rows(o_attn_s.reshape(db, ad), SAMPLE_ROWS).astype(BF16)

        h1, h1_s = _matmul_ws([yconv, o_attn], [yconv_s, o_attn_s], w_out, l, n_out=d, out_dtype=F32,
                              res=xp, res_s=xs, alpha=alpha, name="mm_out")
        x1, x1_bf = _layer_norm(h1, ln[0], ln[1], l)
        x1_s, x1_s_bf = _layer_norm(h1_s, ln[0], ln[1], l)
        up, up_s, w_down_bf = _matmul_ws([x1_bf], [x1_s_bf], w_up, l, n_out=w_up.shape[2], out_dtype=BF16, act="relu2",
                                         w_single_buffer=True, side=w_down, bn=1024, name="mm_up")
        h2 = _matmul(up, w_down_bf, out_dtype=F32, res=x1, alpha=alpha, bk=4096, name="mm_down")
        h2_s = _matmul(up_s, w_down_bf, out_dtype=F32, res=x1_s, alpha=alpha, bk=2048, name="mm_down_s")
        xp, xp_bf = _layer_norm(h2, ln[2], ln[3], l)
        xs, xs_bf = _layer_norm(h2_s, ln[2], ln[3], l)

        kv_shape = (2, N_KV, HEAD_DIM)
        outs["cmp_p"].append(new_cmp.reshape(bsz, t_len, *kv_shape))
        outs["sel_p"].append(new_sel.reshape(bsz, t_len, *kv_shape))
        outs["win_p"].append(new_win.reshape(bsz, t_len, *kv_shape)[:, t_len - min(WINDOW, t_len):])
        outs["conv_p"].append(new_conv)
        outs["cmp_s"].append(cmp_row.reshape(db, 1, *kv_shape))
        outs["sel_s"].append(sel_row.reshape(db, 1, *kv_shape))
        outs["win_s"].append(win_row.reshape(db, *tile))
        outs["conv_s"].append(jnp.stack([state_conv[l, :, 1], u_s], axis=1))

    st = {k: jnp.stack(v) for k, v in outs.items()}
    new_win_s = _roll_window(win_all, st["win_s"].reshape(depth * db, *tile)).reshape(depth, db, wb, 2, N_KV, HEAD_DIM)
    return (xp.reshape(bsz, t_len, d), xs[:db].reshape(db, 1, d), st["cmp_p"], st["sel_p"], st["win_p"], st["conv_p"],
            st["cmp_s"], st["sel_s"], new_win_s, st["conv_s"])
```

```python
import functools
import math

import jax
import jax.numpy as jnp
from jax import lax
from jax.experimental import pallas as pl
from jax.experimental.pallas import tpu as pltpu

F32 = jnp.float32
BF16 = jnp.bfloat16

HEAD_DIM = 128
N_KV = 4
KV_DIM = N_KV * HEAD_DIM
ROT_DIM = HEAD_DIM // 4
ROPE_THETA = 500000.0
BLK_CMP = 32
STRIDE_CMP = 16
BLK_SEL = 64
N_SEL = 16
WINDOW = 512
CONV_W = 3
LN_EPS = 1e-5
FORCE = 1e4
NEG = -1e30
Q_SCALE = HEAD_DIM ** -0.5 * math.log2(math.e)
SAMPLE_ROWS = 16
ATTN_TQ = 256
CAST_CHUNK = 512
CONV_CHUNK = 256

VMEM_LIMIT_BYTES = 62 * 1024 * 1024


def _cparams(sem):
    return pltpu.CompilerParams(dimension_semantics=sem, vmem_limit_bytes=VMEM_LIMIT_BYTES)


def _pick(n, pref):
    if n <= pref:
        return n
    t = pref
    while n % t:
        t //= 2
    return t


def _epilogue(acc, res, *, act, alpha, ln_refs=None):
    if act == "relu2":
        acc = jnp.square(jnp.maximum(acc, 0.0))
    if res is not None:
        if ln_refs is not None:
            stats_ref, g_ref, b_ref = ln_refs
            res = (res - stats_ref[:, 0:1]) * stats_ref[:, 1:2] * g_ref[...] + b_ref[...]
        acc = alpha * res + acc
    return acc


def _mm_ws_kernel(*refs, n_x, ksplit, act, alpha, has_res, has_ln, has_side, w_is_nk):
    w_ref = refs[0]
    x_refs = refs[1:1 + n_x]
    xs_refs = refs[1 + n_x:1 + 2 * n_x]
    pos = 1 + 2 * n_x
    res_ref, ress_ref = (refs[pos], refs[pos + 1]) if has_res else (None, None)
    pos += 2 * has_res
    ln_refs = refs[pos:pos + 3] if has_ln else None
    pos += 3 * has_ln
    side_ref = refs[pos] if has_side else None
    pos += has_side
    o_ref, os_ref = refs[pos], refs[pos + 1]
    wbf_ref = refs[-1]

    if has_side:
        refs[pos + 2][...] = side_ref[...].astype(BF16)

    def product(x_list):
        acc = None
        for x_ref, (k0, k1) in zip(x_list, ksplit):
            part = jnp.dot(x_ref[...], wbf_ref[k0:k1, :], preferred_element_type=F32)
            acc = part if acc is None else acc + part
        return acc

    @pl.when(pl.program_id(1) == 0)
    def _():
        kdim = wbf_ref.shape[0]
        ck = math.gcd(kdim, CAST_CHUNK)
        for k0 in range(0, kdim, ck):
            if w_is_nk:
                wbf_ref[k0:k0 + ck, :] = w_ref[:, k0:k0 + ck].T.astype(BF16)
            else:
                wbf_ref[k0:k0 + ck, :] = w_ref[k0:k0 + ck, :].astype(BF16)
        res_s = ress_ref[...] if has_res else None
        os_ref[...] = _epilogue(product(xs_refs), res_s, act=act, alpha=alpha).astype(os_ref.dtype)

    res = res_ref[...] if has_res else None
    o_ref[...] = _epilogue(product(x_refs), res, act=act, alpha=alpha, ln_refs=ln_refs).astype(o_ref.dtype)


def _ln_specs(res_ln, bm, bn, row_of, col_of):
    _, stats, g, b, ln_layer = res_ln
    vec = lambda *idx: (ln_layer, 0, col_of(*idx))
    specs = [pl.BlockSpec((bm, 2), lambda *idx: (row_of(*idx), 0)),
             pl.BlockSpec((None, 1, bn), vec), pl.BlockSpec((None, 1, bn), vec)]
    return specs, [stats, g, b]


def _matmul_ws(xs, xs_s, w, layer, *, n_out, out_dtype, act=None, res=None, res_s=None, res_ln=None, alpha=1.0,
               w_is_nk=False, w_single_buffer=False, side=None, bm=1024, bn=512, name="mm"):
    if res_ln is not None:
        res = res_ln[0]
    m = xs[0].shape[0]
    ms = xs_s[0].shape[0]
    kdim = w.shape[2] if w_is_nk else w.shape[1]
    bm, bn = _pick(m, bm), _pick(n_out, bn)
    ksplit, k0 = [], 0
    for x in xs:
        ksplit.append((k0, k0 + x.shape[1]))
        k0 += x.shape[1]
    assert k0 == kdim
    w_mode = dict(pipeline_mode=pl.Buffered(1)) if w_single_buffer else {}
    if w_is_nk:
        in_specs = [pl.BlockSpec((None, bn, kdim), lambda j, i: (layer, j, 0), **w_mode)]
    else:
        in_specs = [pl.BlockSpec((None, kdim, bn), lambda j, i: (layer, 0, j), **w_mode)]
    in_specs += [pl.BlockSpec((bm, x.shape[1]), lambda j, i: (i, 0)) for x in xs]
    in_specs += [pl.BlockSpec((ms, x.shape[1]), lambda j, i: (0, 0)) for x in xs_s]
    args = [w, *xs, *xs_s]
    if res is not None:
        in_specs += [pl.BlockSpec((bm, bn), lambda j, i: (i, j)), pl.BlockSpec((ms, bn), lambda j, i: (0, j))]
        args += [res, res_s]
    if res_ln is not None:
        specs, ln_args = _ln_specs(res_ln, bm, bn, lambda j, i: i, lambda j, i: j)
        in_specs += specs
        args += ln_args
    nj, ni = n_out // bn, m // bm
    out_specs = [pl.BlockSpec((bm, bn), lambda j, i: (i, j)), pl.BlockSpec((ms, bn), lambda j, i: (0, j))]
    out_shape = [jax.ShapeDtypeStruct((m, n_out), out_dtype), jax.ShapeDtypeStruct((ms, n_out), out_dtype)]
    if side is not None:
        rows, cols = side.shape[1] // (nj * ni), side.shape[2]
        assert rows * nj * ni == side.shape[1] and rows % 16 == 0
        in_specs.append(pl.BlockSpec((None, rows, cols), lambda j, i: (layer, j * ni + i, 0)))
        args.append(side)
        out_specs.append(pl.BlockSpec((rows, cols), lambda j, i: (j * ni + i, 0)))
        out_shape.append(jax.ShapeDtypeStruct(side.shape[1:], BF16))
    return pl.pallas_call(
        functools.partial(_mm_ws_kernel, n_x=len(xs), ksplit=tuple(ksplit), act=act, alpha=alpha,
                          has_res=res is not None, has_ln=res_ln is not None, has_side=side is not None,
                          w_is_nk=w_is_nk),
        grid=(nj, ni),
        in_specs=in_specs,
        out_specs=out_specs,
        out_shape=out_shape,
        scratch_shapes=[pltpu.VMEM((kdim, bn), BF16)],
        compiler_params=_cparams(("arbitrary", "arbitrary")),
        name=name,
    )(*args)


def _mm_kernel(*refs, nk, act, alpha, has_res, has_ln):
    x_ref, w_ref = refs[0], refs[1]
    res_ref = refs[2] if has_res else None
    ln_refs = refs[3:6] if has_ln else None
    n_in = 2 + has_res + 3 * has_ln
    o_ref = refs[n_in]
    acc_ref = (refs[n_in + 1] if len(refs) > n_in + 1 else o_ref) if nk > 1 else None
    part = jnp.dot(x_ref[...], w_ref[...], preferred_element_type=F32)

    def finish(acc):
        res = res_ref[...] if has_res else None
        o_ref[...] = _epilogue(acc, res, act=act, alpha=alpha, ln_refs=ln_refs).astype(o_ref.dtype)

    if nk == 1:
        finish(part)
    else:
        k = pl.program_id(2)

        @pl.when(k == 0)
        def _():
            acc_ref[...] = part

        @pl.when((k > 0) & (k < nk - 1))
        def _():
            acc_ref[...] += part

        @pl.when(k == nk - 1)
        def _():
            finish(acc_ref[...] + part)


def _matmul(x, w, layer=None, *, out_dtype, act=None, res=None, res_ln=None, alpha=1.0,
            bm=1024, bn=1024, bk=4096, name="mm"):
    if res_ln is not None:
        res = res_ln[0]
    m, kdim = x.shape
    n = w.shape[-1]
    bm, bn, bk = _pick(m, bm), _pick(n, bn), _pick(kdim, bk)
    nk = kdim // bk
    if layer is None:
        w_spec = pl.BlockSpec((bk, bn), lambda i, j, k: (k, j))
    else:
        w_spec = pl.BlockSpec((None, bk, bn), lambda i, j, k: (layer, k, j))
    in_specs = [pl.BlockSpec((bm, bk), lambda i, j, k: (i, k)), w_spec]
    args = [x, w]
    if res is not None:
        in_specs.append(pl.BlockSpec((bm, bn), lambda i, j, k: (i, j)))
        args.append(res)
    if res_ln is not None:
        specs, ln_args = _ln_specs(res_ln, bm, bn, lambda i, j, k: i, lambda i, j, k: j)
        in_specs += specs
        args += ln_args
    return pl.pallas_call(
        functools.partial(_mm_kernel, nk=nk, act=act, alpha=alpha, has_res=res is not None,
                          has_ln=res_ln is not None),
        grid=(m // bm, n // bn, nk),
        in_specs=in_specs,
        out_specs=pl.BlockSpec((bm, bn), lambda i, j, k: (i, j)),
        out_shape=jax.ShapeDtypeStruct((m, n), out_dtype),
        scratch_shapes=[pltpu.VMEM((bm, bn), F32)] if nk > 1 and out_dtype != F32 else [],
        compiler_params=_cparams(("parallel", "parallel", "arbitrary")),
        name=name,
    )(*args)


def _ln_kernel(h_ref, g_ref, b_ref, *out_refs, want_f32):
    obf_ref, stats_ref = out_refs[-2], out_refs[-1]
    h = h_ref[...]
    mu = jnp.mean(h, axis=-1, keepdims=True)
    d = h - mu
    var = jnp.mean(d * d, axis=-1, keepdims=True)
    rstd = lax.rsqrt(var + LN_EPS)
    y = d * rstd * g_ref[...] + b_ref[...]
    if want_f32:
        out_refs[0][...] = y
    obf_ref[...] = y.astype(BF16)
    stats_ref[:, 0:1] = mu
    stats_ref[:, 1:2] = rstd


def _layer_norm(h, g, b, layer, *, want_f32=True, bm=256):
    m, d = h.shape
    bm = _pick(m, bm)
    row = pl.BlockSpec((bm, d), lambda i: (i, 0))
    vec = pl.BlockSpec((None, 1, d), lambda i: (layer, 0, 0))
    out_specs = [row, pl.BlockSpec((bm, 2), lambda i: (i, 0))]
    out_shape = [jax.ShapeDtypeStruct((m, d), BF16), jax.ShapeDtypeStruct((m, 2), F32)]
    if want_f32:
        out_specs.insert(0, row)
        out_shape.insert(0, jax.ShapeDtypeStruct((m, d), F32))
    outs = pl.pallas_call(
        functools.partial(_ln_kernel, want_f32=want_f32),
        grid=(m // bm,),
        in_specs=[row, vec, vec],
        out_specs=out_specs,
        out_shape=out_shape,
        compiler_params=_cparams(("parallel",)),
        name="layer_norm",
    )(h, g, b)
    return (outs[0] if want_f32 else None), outs[-2], outs[-1]


def _rope_tables(pos):
    half = ROT_DIM // 2
    inv = ROPE_THETA ** (-jnp.arange(half, dtype=F32) / half)
    ang = pos.astype(F32)[:, None] * inv[None, :]
    cos, sin = jnp.cos(ang), jnp.sin(ang)
    n = pos.shape[0]
    one = jnp.ones((n, HEAD_DIM - ROT_DIM), F32)
    zero = jnp.zeros((n, HEAD_DIM - ROT_DIM), F32)
    zh = jnp.zeros((n, half), F32)
    c = jnp.concatenate([cos, cos, one], 1)
    s1 = jnp.concatenate([zh, sin, zero], 1)
    s2 = jnp.concatenate([-sin, zh, zero], 1)
    return c, s1, s2


def _rope(x, c, s1, s2):
    return x * c + pltpu.roll(x, ROT_DIM // 2, 1) * s1 + pltpu.roll(x, HEAD_DIM - ROT_DIM // 2, 1) * s2


def _proj_offsets(cd, ad):
    offs = [0, cd, 2 * cd, 3 * cd, 3 * cd + ad]
    for _ in range(5):
        offs.append(offs[-1] + KV_DIM)
    return offs


def _prep_prompt_kernel(proj_ref, c_ref, s1_ref, s2_ref, cw_ref,
                        yconv_ref, qt_ref, cmp_ref, sel_ref, win_ref, selk_ref, selvt_ref, wink_ref, winvt_ref,
                        nconv_ref, carry_ref, *, cd, ad, tt):
    t = pl.program_id(1)
    c, s1, s2 = c_ref[...], s1_ref[...], s2_ref[...]
    offs = _proj_offsets(cd, ad)

    @pl.when(t == 0)
    def _():
        carry_ref[...] = jnp.zeros_like(carry_ref)

    cw = math.gcd(cd, CONV_CHUNK)
    row = lax.broadcasted_iota(jnp.int32, (tt, cw), 0)
    for lo in range(0, cd, cw):
        cols = slice(lo, lo + cw)
        gb = proj_ref[:, offs[0] + lo:offs[0] + lo + cw]
        u = proj_ref[:, offs[1] + lo:offs[1] + lo + cw] * proj_ref[:, offs[2] + lo:offs[2] + lo + cw]
        c0 = carry_ref[0:1, cols]
        c1 = carry_ref[1:2, cols]
        u1 = jnp.where(row == 0, c1, pltpu.roll(u, 1, 0))
        u2 = jnp.where(row == 0, c0, jnp.where(row == 1, c1, pltpu.roll(u, 2, 0)))
        y = cw_ref[0:1, cols] * u2 + cw_ref[1:2, cols] * u1 + cw_ref[2:3, cols] * u
        yconv_ref[:, cols] = (gb * y).astype(BF16)
    tail = (proj_ref[tt - 2:tt, offs[1]:offs[1] + cd] * proj_ref[tt - 2:tt, offs[2]:offs[2] + cd])
    carry_ref[...] = tail
    nconv_ref[...] = tail

    gqa = ad // HEAD_DIM // N_KV
    for h in range(ad // HEAD_DIM):
        lo = offs[3] + h * HEAD_DIM
        g = h % gqa
        qt = (_rope(proj_ref[:, lo:lo + HEAD_DIM], c, s1, s2) * Q_SCALE).T
        qt_ref[h // gqa, :, g * tt:(g + 1) * tt] = qt.astype(BF16)

    for k_off, out_ref, k_ref, vt_ref in ((offs[4], cmp_ref, None, None), (offs[6], sel_ref, selk_ref, selvt_ref),
                                          (offs[8], win_ref, wink_ref, winvt_ref)):
        for h in range(N_KV):
            lo = k_off + h * HEAD_DIM
            kr = _rope(proj_ref[:, lo:lo + HEAD_DIM], c, s1, s2)
            v = proj_ref[:, lo + KV_DIM:lo + KV_DIM + HEAD_DIM]
            out_ref[:, h, :] = kr
            out_ref[:, N_KV + h, :] = v
            if k_ref is not None:
                k_ref[:, h * HEAD_DIM:(h + 1) * HEAD_DIM] = kr.astype(BF16)
                vt_ref[h] = v.T.astype(BF16)


def _prep_prompt(proj, tables, conv_w, layer, *, bsz, t_len, cd, ad, tt):
    m, nmain = proj.shape
    nt = t_len // tt
    gqa = ad // HEAD_DIM // N_KV
    rows = lambda w: pl.BlockSpec((tt, w), lambda b, t: (b * nt + t, 0))
    rows3 = pl.BlockSpec((tt, 2 * N_KV, HEAD_DIM), lambda b, t: (b * nt + t, 0, 0))
    tab = pl.BlockSpec((tt, HEAD_DIM), lambda b, t: (t, 0))
    cols = lambda w: pl.BlockSpec((None, N_KV, HEAD_DIM, w), lambda b, t: (b, 0, 0, t))
    kv_rows = jax.ShapeDtypeStruct((m, 2 * N_KV, HEAD_DIM), F32)
    k_rows = jax.ShapeDtypeStruct((m, KV_DIM), BF16)
    v_cols = jax.ShapeDtypeStruct((bsz, N_KV, HEAD_DIM, t_len), BF16)
    return pl.pallas_call(
        functools.partial(_prep_prompt_kernel, cd=cd, ad=ad, tt=tt),
        grid=(bsz, nt),
        in_specs=[rows(nmain), tab, tab, tab, pl.BlockSpec((None, CONV_W, cd), lambda b, t: (layer, 0, 0))],
        out_specs=[rows(cd), cols(gqa * tt), rows3, rows3, rows3, rows(KV_DIM), cols(tt), rows(KV_DIM), cols(tt),
                   pl.BlockSpec((None, CONV_W - 1, cd), lambda b, t: (b, 0, 0))],
        out_shape=[jax.ShapeDtypeStruct((m, cd), BF16), jax.ShapeDtypeStruct((bsz, N_KV, HEAD_DIM, gqa * t_len), BF16),
                   kv_rows, kv_rows, kv_rows, k_rows, v_cols, k_rows, v_cols,
                   jax.ShapeDtypeStruct((bsz, CONV_W - 1, cd), F32)],
        scratch_shapes=[pltpu.VMEM((CONV_W - 1, cd), F32)],
        compiler_params=_cparams(("parallel", "arbitrary")),
        name="prep_prompt",
    )(proj, *tables, conv_w)


def _prep_sample_kernel(proj_ref, c_ref, s1_ref, s2_ref, cw_ref, b0_ref, b1_ref,
                        yconv_ref, q_ref, cmp_ref, sel_ref, win_ref, u_ref, *, cd, ad):
    c, s1, s2 = c_ref[...], s1_ref[...], s2_ref[...]
    offs = _proj_offsets(cd, ad)
    gb = proj_ref[:, offs[0]:offs[0] + cd]
    u = proj_ref[:, offs[1]:offs[1] + cd] * proj_ref[:, offs[2]:offs[2] + cd]
    y = cw_ref[0:1, :] * b0_ref[...] + cw_ref[1:2, :] * b1_ref[...] + cw_ref[2:3, :] * u
    yconv_ref[...] = gb * y
    u_ref[...] = u
    for h in range(ad // HEAD_DIM):
        lo = offs[3] + h * HEAD_DIM
        q_ref[:, h * HEAD_DIM:(h + 1) * HEAD_DIM] = (_rope(proj_ref[:, lo:lo + HEAD_DIM], c, s1, s2) * Q_SCALE).astype(BF16)
    for k_off, out_ref in ((offs[4], cmp_ref), (offs[6], sel_ref), (offs[8], win_ref)):
        for h in range(N_KV):
            lo = k_off + h * HEAD_DIM
            out_ref[:, h * HEAD_DIM:(h + 1) * HEAD_DIM] = _rope(proj_ref[:, lo:lo + HEAD_DIM], c, s1, s2)
            out_ref[:, KV_DIM + h * HEAD_DIM:KV_DIM + (h + 1) * HEAD_DIM] = proj_ref[:, lo + KV_DIM:lo + KV_DIM + HEAD_DIM]


def _prep_sample(proj, tables, conv_w, buf0, buf1, *, cd, ad):
    db = proj.shape[0]
    kv2 = 2 * KV_DIM
    full = lambda a: pl.BlockSpec(a.shape, lambda i: (0,) * a.ndim)
    ins = [proj, *tables, conv_w, buf0, buf1]
    outs = [jax.ShapeDtypeStruct((db, cd), F32), jax.ShapeDtypeStruct((db, ad), BF16),
            jax.ShapeDtypeStruct((db, kv2), F32), jax.ShapeDtypeStruct((db, kv2), F32),
            jax.ShapeDtypeStruct((db, kv2), F32), jax.ShapeDtypeStruct((db, cd), F32)]
    return pl.pallas_call(
        functools.partial(_prep_sample_kernel, cd=cd, ad=ad),
        grid=(1,),
        in_specs=[full(a) for a in ins],
        out_specs=[full(o) for o in outs],
        out_shape=outs,
        compiler_params=_cparams(("arbitrary",)),
        name="prep_sample",
    )(*ins)


def _cmp12_kernel(pt_ref, *refs, npg):
    del pt_ref
    pages, w_ref, o_ref = refs[:npg], refs[npg], refs[npg + 1]
    cpp = pages[0].shape[0]
    lhs = ([], [])
    for l in range(STRIDE_CMP):
        per_page = [pltpu.einshape("cjd->jcd", pr[:, l, :, :]) for pr in pages]
        for j in range(2):
            lhs[j].append(jnp.concatenate([xt[j * N_KV + h] for h in range(N_KV) for xt in per_page], axis=0))
    for j in range(2):
        rows = jnp.concatenate(lhs[j], axis=1).astype(BF16)
        out = jnp.dot(rows, w_ref[j], preferred_element_type=F32)
        o_ref[j] = out.reshape(N_KV, npg * cpp, 2 * HEAD_DIM)


def _compress_chunks(pages5, page_ids, w12, layer, *, npg=16):
    bsz, n_pages = page_ids.shape
    cpp = pages5.shape[1]
    assert cpp == 8 and pages5.shape[2] == STRIDE_CMP, "one (8, 128) tile holds a row of all chunks of a page"
    npg = _pick(n_pages, npg)

    def page_spec(i):
        return pl.BlockSpec((None, cpp, STRIDE_CMP, 2 * N_KV, HEAD_DIM),
                            lambda b, g, pt: (pt[b * n_pages + g * npg + i], 0, 0, 0, 0))

    return pl.pallas_call(
        functools.partial(_cmp12_kernel, npg=npg),
        grid_spec=pltpu.PrefetchScalarGridSpec(
            num_scalar_prefetch=1,
            grid=(bsz, n_pages // npg),
            in_specs=[page_spec(i) for i in range(npg)]
            + [pl.BlockSpec((None,) + w12.shape[1:], lambda b, g, pt: (layer, 0, 0, 0))],
            out_specs=pl.BlockSpec((None, 2, N_KV, npg * cpp, 2 * HEAD_DIM), lambda b, g, pt: (b, 0, 0, g, 0)),
        ),
        out_shape=jax.ShapeDtypeStruct((bsz, 2, N_KV, n_pages * cpp, 2 * HEAD_DIM), F32),
        compiler_params=_cparams(("parallel", "parallel")),
        name="compress_chunks",
    )(page_ids.reshape(-1), *([pages5] * npg), w12)


def _cmp_finish_kernel(p_ref, pe_ref, w1_ref, w2_ref, w2t_ref, k_ref, vt_ref, *, nc):
    for j in range(2):
        bias = jnp.dot(pe_ref[j], w1_ref[j], preferred_element_type=F32)[0:1]
        for h in range(N_KV):
            p = p_ref[j, h]
            pre = p[:, :HEAD_DIM] + pltpu.roll(p[:, HEAD_DIM:], nc - 1, 0) + bias
            act = jax.nn.gelu(pre).astype(BF16)
            if j == 0:
                k_ref[h] = jnp.dot(act, w2_ref[j], preferred_element_type=F32).astype(BF16)
            else:
                vt_ref[h] = _dot_nt(w2t_ref[j], act).astype(BF16)


def _compress_finish(p12, pe_flat, w1_flat, w2, layer):
    bsz, _, _, nc, _ = p12.shape
    per_layer = lambda a: pl.BlockSpec((None,) + a.shape[1:], lambda b: (layer,) + (0,) * (a.ndim - 1))
    return pl.pallas_call(
        functools.partial(_cmp_finish_kernel, nc=nc),
        grid=(bsz,),
        in_specs=[pl.BlockSpec((None, 2, N_KV, nc, 2 * HEAD_DIM), lambda b: (b, 0, 0, 0, 0)),
                  per_layer(pe_flat), per_layer(w1_flat), per_layer(w2), per_layer(w2)],
        out_specs=[pl.BlockSpec((None, N_KV, nc, HEAD_DIM), lambda b: (b, 0, 0, 0)),
                   pl.BlockSpec((None, N_KV, HEAD_DIM, nc), lambda b: (b, 0, 0, 0))],
        out_shape=[jax.ShapeDtypeStruct((bsz, N_KV, nc, HEAD_DIM), BF16),
                   jax.ShapeDtypeStruct((bsz, N_KV, HEAD_DIM, nc), BF16)],
        compiler_params=_cparams(("parallel",)),
        name="compress_finish",
    )(p12, pe_flat, w1_flat, w2, jnp.swapaxes(w2, 2, 3))


def _dot_nt(a, b):
    return lax.dot_general(a, b, (((1,), (1,)), ((), ())), preferred_element_type=F32)


def _split3(p):
    hi = p.astype(BF16)
    r1 = p - hi.astype(F32)
    mid = r1.astype(BF16)
    return hi, mid, (r1 - mid.astype(F32)).astype(BF16)


def _dot_exact01(p, onehot_bf16):
    return sum(jnp.dot(t, onehot_bf16, preferred_element_type=F32) for t in _split3(p))


def _overlap(n_cmp_pad, n_sel_pad, n_sel, *, cmp_axis):
    shape = (n_cmp_pad, n_sel_pad) if cmp_axis == 0 else (n_sel_pad, n_cmp_pad)
    n = lax.broadcasted_iota(jnp.int32, shape, cmp_axis) * STRIDE_CMP
    j = lax.broadcasted_iota(jnp.int32, shape, 1 - cmp_axis)
    ov = (n < j * BLK_SEL + BLK_SEL) & (n + BLK_CMP > j * BLK_SEL) & (j < n_sel)
    return ov.astype(BF16)


def _select_scores(imp, qpos, n_sel, *, axis):
    j = lax.broadcasted_iota(jnp.int32, imp.shape, axis)
    jq = qpos // BLK_SEL
    valid = j * BLK_SEL <= qpos
    forced = (j == 0) | (j == jq) | (j == jq - 1)
    score = jnp.where(valid, jnp.where(forced, FORCE, imp), -FORCE)
    return jnp.where(j < n_sel, score, -jnp.inf)


def _rank_counts(score, n_sel, *, axis):
    j = lax.broadcasted_iota(jnp.int32, score.shape, axis)
    cnt = jnp.zeros(score.shape, jnp.int32)
    for i in range(n_sel):
        si = score[i:i + 1, :] if axis == 0 else score[:, i:i + 1]
        beats = (si > score) | ((si == score) & (i < j))
        cnt = cnt + beats.astype(jnp.int32)
    return cnt


def _attn_prompt_kernel(qt_ref, kc_ref, vct_ref, ks_ref, vst_ref, kw_ref, vwt_ref, gate_ref, o_ref,
                        m_ref, l_ref, acc_ref, *, tq, tk, t_len, gqa, n_cmp, n_sel):
    i = pl.program_id(2)
    q0 = i * tq
    qpos = q0 + lax.broadcasted_iota(jnp.int32, (1, tq), 1)
    nc = kc_ref.shape[0]
    nsr = 8 * pl.cdiv(n_sel, 8)
    qt = qt_ref[...]

    def per_head(x):
        return jnp.concatenate([x] * gqa, axis=1)

    def head(x, g):
        return x[:, g * tq:(g + 1) * tq]

    def with_ones(vals_t):
        return jnp.concatenate([vals_t, jnp.ones((16, vals_t.shape[1]), BF16)], axis=0)

    n = lax.broadcasted_iota(jnp.int32, (nc, 1), 0)
    bias_c = jnp.where((n * STRIDE_CMP + BLK_CMP - 1 <= qpos) & (n < n_cmp), 0.0, NEG)
    has_block = (qpos >= BLK_CMP - 1).astype(F32)
    s = jnp.dot(kc_ref[...], qt, preferred_element_type=F32) + per_head(bias_c)
    e = jnp.exp2(s - jnp.max(s, axis=0, keepdims=True))
    p_cmp = e * (per_head(has_block) / jnp.sum(e, axis=0, keepdims=True))
    o_cmp = jnp.dot(vct_ref[...], p_cmp.astype(BF16), preferred_element_type=F32)

    def all_started():
        j = lax.broadcasted_iota(jnp.int32, (nsr, tq), 0)
        return jnp.where(j * BLK_SEL <= qpos, 0.0, NEG)

    def top_k():
        p_sum = sum(head(p_cmp, g) for g in range(gqa))
        ov = _overlap(nc, nsr, n_sel, cmp_axis=1)
        imp = sum(jnp.dot(ov, t, preferred_element_type=F32) for t in _split3(p_sum))
        score = _select_scores(imp, qpos, n_sel, axis=0)
        return jnp.where(_rank_counts(score, n_sel, axis=0) < N_SEL, 0.0, NEG)

    sel_bias = lax.cond((q0 + tq - 1) // BLK_SEL + 1 <= N_SEL, all_started, top_k).astype(BF16)

    sel_bias = jnp.concatenate([sel_bias, jnp.zeros((HEAD_DIM - nsr, tq), BF16)], axis=0)
    qt_sel = jnp.concatenate([qt, per_head(sel_bias)], axis=0)

    m_ref[...] = jnp.full(m_ref.shape, NEG, F32)
    l_ref[...] = jnp.zeros(l_ref.shape, F32)
    acc_ref[...] = jnp.zeros(acc_ref.shape, F32)

    def sel_tile(k0, causal):
        blk_of_key = (k0 + lax.broadcasted_iota(jnp.int32, (tk, HEAD_DIM), 0)) // BLK_SEL
        one_hot = (blk_of_key == lax.broadcasted_iota(jnp.int32, (tk, HEAD_DIM), 1)).astype(BF16)
        keys = jnp.concatenate([ks_ref[pl.ds(k0, tk), :], one_hot], axis=1)
        s = jnp.dot(keys, qt_sel, preferred_element_type=F32)
        if causal:
            kpos = k0 + lax.broadcasted_iota(jnp.int32, (tk, 1), 0)
            s = s + per_head(jnp.where(kpos <= qpos, 0.0, NEG))
        m_old = m_ref[...]
        m_new = jnp.maximum(m_old, jnp.max(s, axis=0, keepdims=True))
        a = jnp.exp2(m_old - m_new)
        p = jnp.exp2((s - m_new).astype(BF16))
        pv = jnp.dot(with_ones(vst_ref[:, pl.ds(k0, tk)]), p, preferred_element_type=F32)
        l_ref[...] = a * l_ref[...] + pv[HEAD_DIM:HEAD_DIM + 1]
        acc_ref[...] = a * acc_ref[...] + pv[:HEAD_DIM]
        m_ref[...] = m_new

    n_tiles = (q0 + tq + tk - 1) // tk

    def full_tile(kj, carry):
        sel_tile(pl.multiple_of(kj * tk, tk), causal=False)
        return carry

    lax.fori_loop(0, n_tiles - 1, full_tile, 0)
    sel_tile(pl.multiple_of((n_tiles - 1) * tk, tk), causal=True)
    o_sel = acc_ref[...] / l_ref[...]

    band = min(WINDOW + tq, t_len)
    kstart = pl.multiple_of(jnp.clip(q0 - WINDOW, 0, t_len - band), 128)
    dist = qpos - (kstart + lax.broadcasted_iota(jnp.int32, (band, 1), 0))
    bias_w = jnp.where((dist >= 0) & (dist < WINDOW), 0.0, NEG)
    s = jnp.dot(kw_ref[pl.ds(kstart, band), :], qt, preferred_element_type=F32) + per_head(bias_w)
    e = jnp.exp2((s - jnp.max(s, axis=0, keepdims=True)).astype(BF16))
    pv = jnp.dot(with_ones(vwt_ref[:, pl.ds(kstart, band)]), e, preferred_element_type=F32)
    o_win = pv[:HEAD_DIM] / pv[HEAD_DIM:HEAD_DIM + 1]

    gates_t = jax.nn.sigmoid(gate_ref[...]).T
    for g in range(gqa):
        out_t = (gates_t[3 * g:3 * g + 1] * head(o_cmp, g) + gates_t[3 * g + 1:3 * g + 2] * head(o_sel, g)
                 + gates_t[3 * g + 2:3 * g + 3] * head(o_win, g))
        o_ref[:, g * HEAD_DIM:(g + 1) * HEAD_DIM] = out_t.T.astype(o_ref.dtype)


def _attn_prompt(qt, k_cmp, vt_cmp, sel_k, sel_vt, win_k, win_vt, gates, *, bsz, t_len, gqa, tq, tk=512):
    tk = _pick(t_len, tk)
    assert tk % tq == 0 and -(-t_len // BLK_SEL) <= HEAD_DIM
    nt = t_len // tq
    nc = k_cmp.shape[2]
    n_cmp = (t_len - BLK_CMP) // STRIDE_CMP + 1
    n_sel = -(-t_len // BLK_SEL)
    per_head = lambda r, c: pl.BlockSpec((None, None, r, c), lambda b, h, i: (b, h, 0, 0))
    kspec = pl.BlockSpec((t_len, HEAD_DIM), lambda b, h, i: (b, h))
    return pl.pallas_call(
        functools.partial(_attn_prompt_kernel, tq=tq, tk=tk, t_len=t_len, gqa=gqa, n_cmp=n_cmp, n_sel=n_sel),
        grid=(bsz, N_KV, nt),
        in_specs=[pl.BlockSpec((None, None, HEAD_DIM, gqa * tq), lambda b, h, i: (b, h, 0, i)),
                  per_head(nc, HEAD_DIM), per_head(HEAD_DIM, nc),
                  kspec, per_head(HEAD_DIM, t_len), kspec, per_head(HEAD_DIM, t_len),
                  pl.BlockSpec((tq, HEAD_DIM), lambda b, h, i: (b * nt + i, h))],
        out_specs=pl.BlockSpec((tq, gqa * HEAD_DIM), lambda b, h, i: (b * nt + i, h)),
        out_shape=jax.ShapeDtypeStruct((bsz * t_len, N_KV * gqa * HEAD_DIM), BF16),
        scratch_shapes=[pltpu.VMEM((1, gqa * tq), F32), pltpu.VMEM((1, gqa * tq), F32),
                        pltpu.VMEM((HEAD_DIM, gqa * tq), F32)],
        compiler_params=_cparams(("parallel", "parallel", "arbitrary")),
        name="attn_prompt",
    )(qt, k_cmp, vt_cmp, sel_k, sel_vt, win_k, win_vt, gates)


def _attn_sample_a_kernel(q_ref, kc_ref, vct_ref, win_ref, nw_ref, ocmp_ref, owin_ref, idx_ref,
                          *, gqa, n_cmp, n_sel, qpos, wb):
    h = pl.program_id(1)
    qs = q_ref[...]
    nc = kc_ref.shape[0]

    n = lax.broadcasted_iota(jnp.int32, (1, nc), 1)
    ok = (n * STRIDE_CMP + BLK_CMP - 1 <= qpos) & (n < n_cmp)
    s = jnp.where(ok, _dot_nt(qs, kc_ref[...]), NEG)
    e = jnp.where(ok, jnp.exp2(s - jnp.max(s, axis=-1, keepdims=True)), 0.0)
    p_cmp = e / jnp.maximum(jnp.sum(e, axis=-1, keepdims=True), 1e-30)
    ocmp_ref[...] = _dot_nt(p_cmp.astype(BF16), vct_ref[...])

    psum = jnp.sum(p_cmp, axis=0, keepdims=True)
    nsl = 128 * pl.cdiv(n_sel, 128)
    imp = _dot_exact01(jnp.broadcast_to(psum, (8, nc)), _overlap(nc, nsl, n_sel, cmp_axis=0))[0:1]
    score = _select_scores(imp, jnp.full((1, 1), qpos, jnp.int32), n_sel, axis=1)
    cnt = _rank_counts(score, n_sel, axis=1)
    lane = lax.broadcasted_iota(jnp.int32, (1, nsl), 1)
    slot = lax.broadcasted_iota(jnp.int32, (N_SEL, HEAD_DIM), 0)
    idx = jnp.zeros((N_SEL, HEAD_DIM), jnp.int32)
    for k in range(N_SEL):
        blk = jnp.sum(jnp.where(cnt == k, lane, 0), axis=-1, keepdims=True)
        idx = jnp.where(slot == k, blk, idx)
    idx_ref[...] = idx

    width = 2 * N_KV
    rows = win_ref[...].reshape(wb * width, HEAD_DIM).astype(BF16)
    kn = nw_ref[pl.ds(h, 1), :]
    vn = nw_ref[pl.ds(N_KV + h, 1), :]
    col = lax.broadcasted_iota(jnp.int32, (1, wb * width), 1)
    kp = qpos - wb + col // width
    okb = (col % width == h) & (qpos - kp < WINDOW) & (kp >= 0)
    sb = jnp.where(okb, _dot_nt(qs, rows), NEG)
    sn = jnp.sum(qs.astype(F32) * kn, axis=-1, keepdims=True)
    mx = jnp.maximum(jnp.max(sb, axis=-1, keepdims=True), sn)
    pb = jnp.where(okb, jnp.exp2(sb - mx), 0.0)
    pn = jnp.exp2(sn - mx)
    den = jnp.sum(pb, axis=-1, keepdims=True) + pn
    num = jnp.dot(pltpu.roll(pb, N_KV, 1).astype(BF16), rows, preferred_element_type=F32) + pn * vn
    owin_ref[...] = num / den


def _attn_sample_a(q4, k_cmp, vt_cmp, win_all, new_win8, layer, *, gqa, qpos):
    db = q4.shape[0]
    nc = k_cmp.shape[2]
    wb = win_all.shape[1]
    n_cmp = (qpos + 1 - BLK_CMP) // STRIDE_CMP + 1
    n_sel = -(-(qpos + 1) // BLK_SEL)
    head = pl.BlockSpec((None, None, gqa, HEAD_DIM), lambda b, h: (b, h, 0, 0))
    return pl.pallas_call(
        functools.partial(_attn_sample_a_kernel, gqa=gqa, n_cmp=n_cmp, n_sel=n_sel, qpos=qpos, wb=wb),
        grid=(db, N_KV),
        in_specs=[head, pl.BlockSpec((None, None, nc, HEAD_DIM), lambda b, h: (b, h, 0, 0)),
                  pl.BlockSpec((None, None, HEAD_DIM, nc), lambda b, h: (b, h, 0, 0)),
                  pl.BlockSpec((None, wb, 2 * N_KV, HEAD_DIM), lambda b, h: (layer * db + b, 0, 0, 0)),
                  pl.BlockSpec((None, 2 * N_KV, HEAD_DIM), lambda b, h: (b, 0, 0))],
        out_specs=[head, head, pl.BlockSpec((None, None, N_SEL, HEAD_DIM), lambda b, h: (b, h, 0, 0))],
        out_shape=[jax.ShapeDtypeStruct((db, N_KV, gqa, HEAD_DIM), F32),
                   jax.ShapeDtypeStruct((db, N_KV, gqa, HEAD_DIM), F32),
                   jax.ShapeDtypeStruct((db, N_KV, N_SEL, HEAD_DIM), jnp.int32)],
        compiler_params=_cparams(("parallel", "parallel")),
        name="attn_sample_cmp_win",
    )(q4, k_cmp, vt_cmp, win_all, new_win8)


def _attn_sample_b_kernel(idx_ref, phys_ref, *refs, gqa, qpos, n_past_blocks):
    del phys_ref
    q_ref = refs[0]
    blk_refs = refs[1:1 + N_SEL]
    ns_ref, ocmp_ref, owin_ref, gate_ref, o_ref = refs[1 + N_SEL:]
    b, h = pl.program_id(0), pl.program_id(1)
    qs = q_ref[...]
    width = 2 * N_KV
    ncol = BLK_SEL * width
    col = lax.broadcasted_iota(jnp.int32, (1, ncol), 1)

    kn = ns_ref[pl.ds(h, 1), :]
    vn = ns_ref[pl.ds(N_KV + h, 1), :]
    sn = jnp.sum(qs.astype(F32) * kn, axis=-1, keepdims=True)

    scores, mx = [], sn
    for k in range(N_SEL):
        blk = idx_ref[(b * N_KV + h) * N_SEL + k]
        ok = (col % width == h) & (blk * BLK_SEL + col // width <= qpos) & (blk < n_past_blocks)
        s = jnp.where(ok, _dot_nt(qs, blk_refs[k][...].reshape(ncol, HEAD_DIM).astype(BF16)), NEG)
        scores.append(s)
        mx = jnp.maximum(mx, jnp.max(s, axis=-1, keepdims=True))
    pn = jnp.exp2(sn - mx)
    den, num = pn, pn * vn
    for k in range(N_SEL):
        p = jnp.exp2(scores[k] - mx)
        den = den + jnp.sum(p, axis=-1, keepdims=True)
        num = num + jnp.dot(pltpu.roll(p, N_KV, 1).astype(BF16),
                            blk_refs[k][...].reshape(ncol, HEAD_DIM).astype(BF16), preferred_element_type=F32)
    o_sel = num / den

    gates = jax.nn.sigmoid(gate_ref[...])
    o_cmp, o_win = ocmp_ref[...], owin_ref[...]
    rows = []
    for g in range(gqa):
        rows.append(gates[:, 3 * g:3 * g + 1] * o_cmp[g:g + 1] + gates[:, 3 * g + 1:3 * g + 2] * o_sel[g:g + 1]
                    + gates[:, 3 * g + 2:3 * g + 3] * o_win[g:g + 1])
    o_ref[...] = jnp.concatenate(rows, axis=0)


def _attn_sample_b(idx, phys, q4, sel_blocks, new_sel8, o_cmp, o_win, gates4, *, gqa, qpos, n_past_blocks):
    db = q4.shape[0]
    head = pl.BlockSpec((None, None, gqa, HEAD_DIM), lambda b, h, i, p: (b, h, 0, 0))

    def blk_spec(k):
        return pl.BlockSpec((None, BLK_SEL, 2 * N_KV, HEAD_DIM),
                            lambda b, h, i, p: (p[(b * N_KV + h) * N_SEL + k], 0, 0, 0))

    return pl.pallas_call(
        functools.partial(_attn_sample_b_kernel, gqa=gqa, qpos=qpos, n_past_blocks=n_past_blocks),
        grid_spec=pltpu.PrefetchScalarGridSpec(
            num_scalar_prefetch=2,
            grid=(db, N_KV),
            in_specs=[head] + [blk_spec(k) for k in range(N_SEL)]
            + [pl.BlockSpec((None, 2 * N_KV, HEAD_DIM), lambda b, h, i, p: (b, 0, 0)),
               head, head,
               pl.BlockSpec((None, None, 1, HEAD_DIM), lambda b, h, i, p: (b, h, 0, 0))],
            out_specs=head,
        ),
        out_shape=jax.ShapeDtypeStruct((db, N_KV, gqa, HEAD_DIM), F32),
        compiler_params=_cparams(("parallel", "parallel")),
        name="attn_sample_sel",
    )(idx.reshape(-1), phys.reshape(-1), q4, *([sel_blocks] * N_SEL), new_sel8, o_cmp, o_win, gates4)


def _roll_window_kernel(s_ref, n_ref, o_ref, *, wb):
    o_ref[0:wb - 1] = s_ref[1:wb]
    o_ref[wb - 1] = n_ref[...]


def _roll_window(state, new_rows):
    n, wb = state.shape[:2]
    tile = (2 * N_KV, HEAD_DIM)
    return pl.pallas_call(
        functools.partial(_roll_window_kernel, wb=wb),
        grid=(n,),
        in_specs=[pl.BlockSpec((None, wb) + tile, lambda b: (b, 0, 0, 0)), pl.BlockSpec((None,) + tile, lambda b: (b, 0, 0))],
        out_specs=pl.BlockSpec((None, wb) + tile, lambda b: (b, 0, 0, 0)),
        out_shape=jax.ShapeDtypeStruct(state.shape, state.dtype),
        compiler_params=_cparams(("parallel",)),
        name="roll_window",
    )(state, new_rows)


def _pad_rows(a, rows):
    return jnp.pad(a, ((0, rows - a.shape[0]), (0, 0)))


def kernel(x_prompt, x_sample, cache_cmp, cache_sel, state_win, state_conv, page_table, w_in, conv_w, cmp_pe, cmp_w1, cmp_w2, w_out, ln1_g, ln1_b, w_up, w_down, ln2_g, ln2_b):
    bsz, t_len, d = x_prompt.shape
    db, dec_seq, _ = x_sample.shape
    depth = w_in.shape[0]
    cd = state_conv.shape[-1]
    ad = d - cd
    gqa = ad // HEAD_DIM // N_KV
    pool, page_rows = cache_cmp.shape[1], cache_cmp.shape[2]
    n_pages = page_table.shape[1]
    past = n_pages * page_rows
    wb = state_win.shape[2]
    assert dec_seq == 1 and wb == WINDOW and t_len >= WINDOW and 3 * gqa <= HEAD_DIM and db <= SAMPLE_ROWS
    assert page_rows % BLK_SEL == 0 and t_len % page_rows == 0
    alpha = (2.0 * depth) ** 0.25
    nmain = 3 * cd + ad + 6 * KV_DIM
    cpp = page_rows // STRIDE_CMP
    bpp = page_rows // BLK_SEL
    tile = (2 * N_KV, HEAD_DIM)
    tq = _pick(t_len, ATTN_TQ)

    tab_p = _rope_tables(jnp.arange(t_len, dtype=jnp.int32))
    tab_s = _rope_tables(jnp.full((db,), past, jnp.int32))
    pt_prompt = jnp.arange(bsz * (t_len // page_rows), dtype=jnp.int32).reshape(bsz, t_len // page_rows)

    wg = w_in[:, :, nmain:].reshape(depth, d, N_KV, 3 * gqa)
    wg = jnp.pad(wg, ((0, 0), (0, 0), (0, 0), (0, HEAD_DIM - 3 * gqa))).reshape(depth, d, KV_DIM).astype(BF16)
    half = STRIDE_CMP * HEAD_DIM
    w1_flat = cmp_w1.reshape(depth, 2, BLK_CMP * HEAD_DIM, HEAD_DIM).astype(BF16)
    w12 = jnp.concatenate([w1_flat[:, :, :half], w1_flat[:, :, half:]], axis=3)
    pe_flat = jnp.broadcast_to(cmp_pe.reshape(depth, 2, 1, BLK_CMP * HEAD_DIM), (depth, 2, 8, BLK_CMP * HEAD_DIM)).astype(BF16)
    w2 = cmp_w2.astype(BF16)
    w_in_nk = jnp.swapaxes(w_in, 1, 2)
    ln = [a.reshape(depth, 1, d) for a in (ln1_g, ln1_b, ln2_g, ln2_b)]

    cmp_pages = cache_cmp.reshape(depth * pool, cpp, STRIDE_CMP, *tile)
    sel_blocks = cache_sel.reshape(depth * pool * bpp, BLK_SEL, *tile)
    win_all = state_win.reshape(depth * db, wb, *tile)

    xp = x_prompt.reshape(bsz * t_len, d)
    xp_ln = None
    xp_bf = xp.astype(BF16)
    xs = _pad_rows(x_sample.reshape(db, d), SAMPLE_ROWS)
    xs_bf = xs.astype(BF16)

    outs = {k: [] for k in ("cmp_p", "sel_p", "win_p", "conv_p", "cmp_s", "sel_s", "win_s", "conv_s")}
    for l in range(depth):
        proj, proj_s = _matmul_ws([xp_bf], [xs_bf], w_in_nk, l, n_out=nmain, out_dtype=F32, w_is_nk=True,
                                  w_single_buffer=True, bn=1024, name="mm_in")
        gates = _matmul(xp_bf, wg, l, out_dtype=F32, bn=KV_DIM, name="mm_gate")
        gates_s = _matmul(xs_bf, wg, l, out_dtype=F32, bn=KV_DIM, name="mm_gate_s")[:db]

        yconv, qt, new_cmp, new_sel, new_win, sel_k, sel_vt, win_k, win_vt, new_conv = _prep_prompt(
            proj, tab_p, conv_w, l, bsz=bsz, t_len=t_len, cd=cd, ad=ad, tt=tq)
        p12 = _compress_chunks(new_cmp.reshape(-1, cpp, STRIDE_CMP, *tile), pt_prompt, w12, l)
        k_cmp, vt_cmp = _compress_finish(p12, pe_flat, w1_flat, w2, l)
        o_attn = _attn_prompt(qt, k_cmp, vt_cmp, sel_k, sel_vt, win_k, win_vt, gates,
                              bsz=bsz, t_len=t_len, gqa=gqa, tq=tq)

        yconv_s, q_s, cmp_row, sel_row, win_row, u_s = _prep_sample(
            proj_s[:db], tab_s, conv_w[l], state_conv[l, :, 0], state_conv[l, :, 1], cd=cd, ad=ad)
        p12_s = _compress_chunks(cmp_pages, page_table + l * pool, w12, l)
        k_cmp_s, vt_cmp_s = _compress_finish(p12_s, pe_flat, w1_flat, w2, l)
        q4 = q_s.reshape(db, N_KV, gqa, HEAD_DIM)
        o_cmp_s, o_win_s, idx = _attn_sample_a(q4, k_cmp_s, vt_cmp_s, win_all, win_row.reshape(db, *tile), l,
                                               gqa=gqa, qpos=past)
        idx = idx[..., 0]
        page = jnp.minimum(idx // bpp, n_pages - 1).reshape(db, N_KV * N_SEL)
        phys = (jnp.take_along_axis(page_table, page, axis=1) + l * pool) * bpp + idx.reshape(db, -1) % bpp
        o_attn_s = _attn_sample_b(idx, phys, q4, sel_blocks, sel_row.reshape(db, *tile), o_cmp_s, o_win_s,
                                  gates_s.reshape(db, N_KV, 1, HEAD_DIM), gqa=gqa, qpos=past, n_past_blocks=n_pages * bpp)
        yconv_s = _pad_rows(yconv_s, SAMPLE_ROWS).astype(BF16)
        o_attn_s = _pad_rows(o_attn_s.reshape(db, ad), SAMPLE_ROWS).astype(BF16)

        h1, h1_s = _matmul_ws([yconv, o_attn], [yconv_s, o_attn_s], w_out, l, n_out=d, out_dtype=F32,
                              res=xp if xp_ln is None else None, res_ln=xp_ln, res_s=xs, alpha=alpha, name="mm_out")
        _, x1_bf, st1 = _layer_norm(h1, ln[0], ln[1], l, want_f32=False)
        x1_s, x1_s_bf, _ = _layer_norm(h1_s, ln[0], ln[1], l)
        up, up_s, w_down_bf = _matmul_ws([x1_bf], [x1_s_bf], w_up, l, n_out=w_up.shape[2], out_dtype=BF16, act="relu2",
                                         w_single_buffer=True, side=w_down, bn=1024, name="mm_up")
        h2 = _matmul(up, w_down_bf, out_dtype=F32, res_ln=(h1, st1, ln[0], ln[1], l), alpha=alpha, bk=4096, name="mm_down")
        h2_s = _matmul(up_s, w_down_bf, out_dtype=F32, res=x1_s, alpha=alpha, bk=2048, name="mm_down_s")
        last = l == depth - 1
        xp, xp_bf, st2 = _layer_norm(h2, ln[2], ln[3], l, want_f32=last)
        xp_ln = None if last else (h2, st2, ln[2], ln[3], l)
        xs, xs_bf, _ = _layer_norm(h2_s, ln[2], ln[3], l)

        kv_shape = (2, N_KV, HEAD_DIM)
        outs["cmp_p"].append(new_cmp.reshape(bsz, t_len, *kv_shape))
        outs["sel_p"].append(new_sel.reshape(bsz, t_len, *kv_shape))
        outs["win_p"].append(new_win.reshape(bsz, t_len, *kv_shape)[:, t_len - min(WINDOW, t_len):])
        outs["conv_p"].append(new_conv)
        outs["cmp_s"].append(cmp_row.reshape(db, 1, *kv_shape))
        outs["sel_s"].append(sel_row.reshape(db, 1, *kv_shape))
        outs["win_s"].append(win_row.reshape(db, *tile))
        outs["conv_s"].append(jnp.stack([state_conv[l, :, 1], u_s], axis=1))

    st = {k: jnp.stack(v) for k, v in outs.items()}
    new_win_s = _roll_window(win_all, st["win_s"].reshape(depth * db, *tile)).reshape(depth, db, wb, 2, N_KV, HEAD_DIM)
    return (xp.reshape(bsz, t_len, d), xs[:db].reshape(db, 1, d), st["cmp_p"], st["sel_p"], st["win_p"], st["conv_p"],
            st["cmp_s"], st["sel_s"], new_win_s, st["conv_s"])
```
